```python
import math
import jax, jax.numpy as jnp
from jax import lax
import numpy as np

D_MODEL = 2048
BATCH = 16
SEQ = 2048
DEPTH = 2

GRID_W = 64
CTX_LEN = 256
N_ADA = 6
EPS = 1e-6
GDN_DK = 128
GDN_DV = 128
GDN_HEADS = (D_MODEL // 2) // GDN_DV
GDN_CHUNK = 64
SHORT_CONV = 5
GDN_QK = GDN_HEADS * GDN_DK
GDN_VW = GDN_HEADS * GDN_DV
GDN_QKV = 2 * GDN_QK + GDN_VW
GDN_IN = GDN_QKV + 4 * GDN_HEADS
POOL_GROUPS = 4
POOL_RADII = (1, 2, 4, 8)
POOL_W = D_MODEL // 2
POOL_GC = POOL_W // POOL_GROUPS
EVEN_IN = GDN_IN + GDN_VW + POOL_W
EVEN_MIX = GDN_VW + POOL_W
SGU_W = D_MODEL
SGU_GROUPS = 4
SGU_CHUNK = 128
D_FF = 4 * D_MODEL
N_EVEN = (DEPTH + 1) // 2
N_ODD = DEPTH // 2
ADA_SCALE = 0.5

kernel_name = "hybrid_gdn_pool_sgu_dit_block"

F32 = jnp.float32


def _rmsnorm(x, g):
    xf = x.astype(F32)
    y = xf * lax.rsqrt(jnp.mean(xf * xf, axis=-1, keepdims=True) + EPS)
    return (y * g.astype(F32)).astype(x.dtype)


def _layernorm(x, g, b):
    xf = x.astype(F32)
    mu = jnp.mean(xf, axis=-1, keepdims=True)
    var = jnp.mean(jnp.square(xf - mu), axis=-1, keepdims=True)
    return ((xf - mu) * lax.rsqrt(var + EPS) * g.astype(F32) + b.astype(F32)).astype(x.dtype)


def _l2norm(t):
    tf = t.astype(F32)
    return tf * lax.rsqrt(jnp.sum(tf * tf, axis=-1, keepdims=True) + EPS)


def _ada(cvec, w, b):
    mod = jax.nn.silu(cvec) @ w + b
    return mod.reshape(cvec.shape[:-1] + (N_ADA, w.shape[0]))


def _modulate(h, g, shift, scale):
    return _rmsnorm(h, g) * (1 + scale) + shift


def _grid_pos_embed(L, d):
    rows = L // GRID_W
    nf = d // 4
    omega = 1.0 / (10000.0 ** (jnp.arange(nf, dtype=F32) / nf))
    r = jnp.repeat(jnp.arange(rows, dtype=F32), GRID_W)
    col = jnp.tile(jnp.arange(GRID_W, dtype=F32), rows)
    ar = r[:, None] * omega
    ac = col[:, None] * omega
    return jnp.concatenate([jnp.sin(ar), jnp.cos(ar), jnp.sin(ac), jnp.cos(ac)], axis=-1)


def _short_conv(x, w):
    pad = (SHORT_CONV - 1) // 2
    return lax.conv_general_dilated(
        x, w[:, None, :].astype(x.dtype), window_strides=(1,), padding=[(pad, pad)],
        dimension_numbers=("NWC", "WIO", "NWC"), feature_group_count=x.shape[-1])


def _gdn_inputs(p, conv_w):
    B, L, _ = p.shape
    qkv = jax.nn.silu(_short_conv(p[..., :GDN_QKV], conv_w))
    q = _l2norm(qkv[..., :GDN_QK].reshape(B, L, GDN_HEADS, GDN_DK)) * (GDN_DK ** -0.5)
    k = _l2norm(qkv[..., GDN_QK:2 * GDN_QK].reshape(B, L, GDN_HEADS, GDN_DK))
    v = qkv[..., 2 * GDN_QK:].reshape(B, L, GDN_HEADS, GDN_DV)
    ab = p[..., GDN_QKV:].reshape(B, L, 2, 2, GDN_HEADS)
    return q, k, v, ab[:, :, 0], ab[:, :, 1]


def _gdn_chunk_scan(q, k, v, g, beta, s0, with_output):
    B, L, H, DK = q.shape
    DV = v.shape[-1]
    C = GDN_CHUNK
    N = L // C

    def blk(t):
        t = t.astype(F32).reshape((B, N, C, H) + t.shape[3:])
        return jnp.moveaxis(t, (1, 3), (0, 2))

    qc, kc, vc, gc, bc = blk(q), blk(k), blk(v), blk(g), blk(beta)
    gam = jnp.cumsum(gc, axis=-1)
    idx = jnp.arange(C)
    incl = idx[:, None] >= idx[None, :]
    strict = idx[:, None] > idx[None, :]
    diff = gam[..., :, None] - gam[..., None, :]
    dmat = jnp.where(incl, jnp.exp(jnp.where(incl, diff, 0.0)), 0.0)
    kb = kc * bc[..., None]
    a_mat = jnp.where(strict, jnp.einsum("nbhid,nbhjd->nbhij", kb, kc) * dmat, 0.0) + jnp.eye(C, dtype=F32)
    rhs = jnp.concatenate([vc * bc[..., None], kb * jnp.exp(gam)[..., None]], axis=-1)
    sol = lax.linalg.triangular_solve(a_mat, rhs, left_side=True, lower=True, unit_diagonal=True)
    u, w = sol[..., :DV], sol[..., DV:]
    g_last = gam[..., -1]
    k_dec = kc * jnp.exp(g_last[..., None] - gam)[..., None]
    d_last = jnp.exp(g_last)
    s0 = s0.astype(F32)
    if with_output:
        q_dec = qc * jnp.exp(gam)[..., None]
        qk = jnp.where(incl, jnp.einsum("nbhid,nbhjd->nbhij", qc, kc) * dmat, 0.0)

        def step(s, xs):
            u_n, w_n, kd_n, dl_n, qd_n, qk_n = xs
            v_new = u_n - jnp.einsum("bhck,bhkv->bhcv", w_n, s)
            o = jnp.einsum("bhck,bhkv->bhcv", qd_n, s) + jnp.einsum("bhij,bhjv->bhiv", qk_n, v_new)
            s = s * dl_n[..., None, None] + jnp.einsum("bhck,bhcv->bhkv", kd_n, v_new)
            return s, o

        s_fin, o = lax.scan(step, s0, (u, w, k_dec, d_last, q_dec, qk))
        o = jnp.moveaxis(o, (0, 2), (1, 3)).reshape(B, L, H, DV)
        return o, s_fin

    def step_state(s, xs):
        u_n, w_n, kd_n, dl_n = xs
        v_new = u_n - jnp.einsum("bhck,bhkv->bhcv", w_n, s)
        s = s * dl_n[..., None, None] + jnp.einsum("bhck,bhcv->bhkv", kd_n, v_new)
        return s, None

    s_fin, _ = lax.scan(step_state, s0, (u, w, k_dec, d_last))
    return None, s_fin


def _gdn_bidir(q, k, v, a, b, a_log, dt_bias, s0f, s0b, with_output):
    g = -jnp.exp(a_log.astype(F32)) * jax.nn.softplus(a.astype(F32) + dt_bias.astype(F32))
    beta = jax.nn.sigmoid(b.astype(F32))
    rev = lambda t: jnp.flip(t, axis=1)
    of, sf = _gdn_chunk_scan(q, k, v, g[:, :, 0], beta[:, :, 0], s0f, with_output)
    ob, sb = _gdn_chunk_scan(rev(q), rev(k), rev(v), rev(g[:, :, 1]), rev(beta[:, :, 1]), s0b, with_output)
    if with_output:
        return of + rev(ob), sf, sb
    return None, sf, sb


def _zero_state(batch):
    return jnp.zeros((batch, GDN_HEADS, GDN_DK, GDN_DV), F32)


def _multiscale_pool(h):
    B, L, _ = h.shape
    hf = h.astype(F32)
    csum = jnp.pad(jnp.cumsum(hf, axis=1), ((0, 0), (1, 0), (0, 0)))
    t = jnp.arange(L)
    means = []
    for gi, r in enumerate(POOL_RADII):
        hi = jnp.minimum(t + r + 1, L)
        lo = jnp.maximum(t - r, 0)
        seg = csum[..., gi * POOL_GC:(gi + 1) * POOL_GC]
        means.append((seg[:, hi] - seg[:, lo]) / (hi - lo).astype(F32)[:, None])
    return (jnp.concatenate(means, axis=-1) - hf).astype(h.dtype)


def _even_mixer(h, w_in, conv_w, a_log, dt_bias, onorm_g, pool_w, pool_scale, w_out, s0f, s0b):
    B, L, _ = h.shape
    p = h @ w_in
    q, k, v, a, b = _gdn_inputs(p[..., :GDN_IN], conv_w)
    o, sf, sb = _gdn_bidir(q, k, v, a, b, a_log, dt_bias, s0f, s0b, True)
    z = p[..., GDN_IN:GDN_IN + GDN_VW].reshape(B, L, GDN_HEADS, GDN_DV)
    y_a = (_rmsnorm(o, onorm_g) * jax.nn.silu(z.astype(F32))).reshape(B, L, GDN_VW).astype(h.dtype)
    pooled = _multiscale_pool(p[..., GDN_IN + GDN_VW:]).reshape(B, L, POOL_GROUPS, POOL_GC)
    y_b = jnp.einsum("blgc,gcd->blgd", pooled, pool_w).reshape(B, L, POOL_W) * pool_scale
    y = jnp.concatenate([y_a, y_b], axis=-1) @ w_out
    return y, sf, sb


def _context_states(hc, w_in, conv_w, a_log, dt_bias):
    p = hc @ w_in[:, :GDN_IN]
    q, k, v, a, b = _gdn_inputs(p, conv_w)
    z0 = _zero_state(hc.shape[0])
    _, sf, sb = _gdn_bidir(q, k, v, a, b, a_log, dt_bias, z0, z0, False)
    return sf, sb


def _odd_mixer(h, w_in, ln_g, ln_b, w_s, b_s, w_out):
    B, L, _ = h.shape
    zz = jax.nn.gelu(h @ w_in, approximate=False)
    u, v = zz[..., :SGU_W], zz[..., SGU_W:]
    v = _layernorm(v, ln_g, ln_b).reshape(B, L // SGU_CHUNK, SGU_CHUNK, SGU_GROUPS, SGU_W // SGU_GROUPS)
    mixed = jnp.einsum("bnsgc,gts->bntgc", v, w_s) + b_s.T[None, None, :, :, None]
    return (u * mixed.reshape(B, L, SGU_W)) @ w_out


def _mlp(h, w1, w2):
    return jnp.square(jax.nn.relu(h @ w1)) @ w2


def setup_inputs(seed: int = 0) -> dict:
    key = jax.random.key(seed)
    ks = jax.random.split(key, 32)
    D = D_MODEL

    def nrm(k, shape, fan):
        return jax.random.normal(k, shape, F32) * (fan ** -0.5)

    def gain(k, shape):
        return 1.0 + 0.05 * jax.random.normal(k, shape, F32)

    dt = jnp.exp(jax.random.uniform(ks[13], (N_EVEN, 2, GDN_HEADS), F32,
                                    math.log(1e-3), math.log(1e-1)))
    return {
        "x": jax.random.normal(ks[0], (BATCH, SEQ, D), F32),
        "c": jax.random.normal(ks[1], (BATCH, D), F32),
        "ctx": jax.random.normal(ks[2], (BATCH, CTX_LEN, D), F32),
        "c_ctx": jax.random.normal(ks[3], (D,), F32),
        "ada_w": nrm(ks[4], (DEPTH, D, N_ADA * D), D) * ADA_SCALE,
        "ada_b": 0.01 * jax.random.normal(ks[5], (DEPTH, N_ADA * D), F32),
        "norm1_g": gain(ks[6], (DEPTH, D)),
        "norm2_g": gain(ks[7], (DEPTH, D)),
        "mlp_w1": nrm(ks[8], (DEPTH, D, D_FF), D),
        "mlp_w2": nrm(ks[9], (DEPTH, D_FF, D), D_FF),
        "ev_w_in": nrm(ks[10], (N_EVEN, D, EVEN_IN), D),
        "ev_conv_w": nrm(ks[11], (N_EVEN, SHORT_CONV, GDN_QKV), SHORT_CONV),
        "ev_a_log": jnp.log(jax.random.uniform(ks[12], (N_EVEN, 2, GDN_HEADS), F32, 1.0, 16.0)),
        "ev_dt_bias": dt + jnp.log(-jnp.expm1(-dt)),
        "ev_onorm_g": gain(ks[14], (N_EVEN, GDN_DV)),
        "ev_pool_w": nrm(ks[15], (N_EVEN, POOL_GROUPS, POOL_GC, POOL_GC), POOL_GC),
        "ev_pool_scale": gain(ks[16], (N_EVEN, POOL_W)),
        "ev_w_out": nrm(ks[17], (N_EVEN, EVEN_MIX, D), EVEN_MIX),
        "od_w_in": nrm(ks[18], (N_ODD, D, 2 * SGU_W), D),
        "od_ln_g": gain(ks[19], (N_ODD, SGU_W)),
        "od_ln_b": 0.01 * jax.random.normal(ks[20], (N_ODD, SGU_W), F32),
        "od_ws": nrm(ks[21], (N_ODD, SGU_GROUPS, SGU_CHUNK, SGU_CHUNK), SGU_CHUNK),
        "od_bs": 1.0 + 0.02 * jax.random.normal(ks[22], (N_ODD, SGU_GROUPS, SGU_CHUNK), F32),
        "od_w_out": nrm(ks[23], (N_ODD, SGU_W, D), SGU_W),
        "final_g": gain(ks[24], (D,)),
    }


def reference(x, c, ctx, c_ctx, ada_w, ada_b, norm1_g, norm2_g, mlp_w1, mlp_w2,
              ev_w_in, ev_conv_w, ev_a_log, ev_dt_bias, ev_onorm_g, ev_pool_w, ev_pool_scale, ev_w_out,
              od_w_in, od_ln_g, od_ln_b, od_ws, od_bs, od_w_out, final_g):
    B, L, D = x.shape
    h_x = x + _grid_pos_embed(L, D).astype(x.dtype)[None]
    h_c = ctx
    last_even = 2 * ((DEPTH - 1) // 2)
    for i in range(DEPTH):
        j = i // 2
        mx = _ada(c, ada_w[i], ada_b[i])[:, None]
        mc = _ada(c_ctx, ada_w[i], ada_b[i])
        ctx_full = i < last_even
        hx = _modulate(h_x, norm1_g[i], mx[..., 0, :], mx[..., 1, :])
        if i % 2 == 0:
            ev = (ev_w_in[j], ev_conv_w[j], ev_a_log[j], ev_dt_bias[j], ev_onorm_g[j],
                  ev_pool_w[j], ev_pool_scale[j], ev_w_out[j])
            hc = _modulate(h_c, norm1_g[i], mc[0], mc[1])
            if ctx_full:
                z0 = _zero_state(B)
                yc, sf, sb = _even_mixer(hc, *ev, z0, z0)
            else:
                sf, sb = _context_states(hc, ev_w_in[j], ev_conv_w[j], ev_a_log[j], ev_dt_bias[j])
            yx, _, _ = _even_mixer(hx, *ev, sf, sb)
        else:
            od = (od_w_in[j], od_ln_g[j], od_ln_b[j], od_ws[j], od_bs[j], od_w_out[j])
            yx = _odd_mixer(hx, *od)
            if ctx_full:
                yc = _odd_mixer(_modulate(h_c, norm1_g[i], mc[0], mc[1]), *od)
        h_x = h_x + mx[..., 2, :] * yx
        h_x = h_x + mx[..., 5, :] * _mlp(_modulate(h_x, norm2_g[i], mx[..., 3, :], mx[..., 4, :]),
                                         mlp_w1[i], mlp_w2[i])
        if ctx_full:
            h_c = h_c + mc[2] * yc
            h_c = h_c + mc[5] * _mlp(_modulate(h_c, norm2_g[i], mc[3], mc[4]), mlp_w1[i], mlp_w2[i])
    return _rmsnorm(h_x, final_g)
```

```python
import functools
import math

import jax
import jax.numpy as jnp
from jax import lax
from jax.experimental import pallas as pl
from jax.experimental.pallas import tpu as pltpu

F32 = jnp.float32
BF16 = jnp.bfloat16

EPS = 1e-6
GRID_W = 64
N_ADA = 6
HEAD_DIM = 128
GDN_CHUNK = 64
SHORT_CONV = 5
POOL_RADII = (1, 2, 4, 8)
SGU_GROUPS = 4
SGU_CHUNK = 128

LANES = 128
MOD_ROWS = 32
VMEM_LIMIT = 56 * 1024 * 1024


def _params(*sem):
    return pltpu.CompilerParams(dimension_semantics=sem, vmem_limit_bytes=VMEM_LIMIT)


def _silu(x):
    return x * jax.nn.sigmoid(x)


def _ada_kernel(c_ref, w_ref, b_ref, o_ref):
    s = _silu(c_ref[...]).astype(BF16)
    o_ref[...] = jnp.dot(s, w_ref[...].astype(BF16), preferred_element_type=F32) + b_ref[...]


def _ada_call(cc, ada_w, ada_b):
    depth, d, n = ada_w.shape
    tn = 1024
    return pl.pallas_call(
        _ada_kernel,
        grid=(depth, n // tn),
        in_specs=[
            pl.BlockSpec((MOD_ROWS, d), lambda l, j: (0, 0)),
            pl.BlockSpec((None, d, tn), lambda l, j: (l, 0, j)),
            pl.BlockSpec((None, 1, tn), lambda l, j: (l, 0, j)),
        ],
        out_specs=pl.BlockSpec((None, MOD_ROWS, tn), lambda l, j: (l, 0, j)),
        out_shape=jax.ShapeDtypeStruct((depth, MOD_ROWS, n), F32),
        compiler_params=_params("arbitrary", "arbitrary"),
        name="ada",
    )(cc, ada_w, ada_b.reshape(depth, 1, n))


def _act(y, act):
    if act == "relu2":
        r = jnp.maximum(y, 0.0)
        return r * r
    if act == "gelu":
        return 0.5 * y * (1.0 + lax.erf(y * math.sqrt(0.5)))
    return y


def _nm_kernel(*refs, has_pos, act, row_chunk):
    if has_pos:
        x_ref, pos_ref, g_ref, sh_ref, sc_ref, w_ref, o_ref, h_ref, xn_ref = refs
    else:
        x_ref, g_ref, sh_ref, sc_ref, w_ref, o_ref, xn_ref = refs
    tm = x_ref.shape[0]

    @pl.when(pl.program_id(1) == 0)
    def _():
        gain = g_ref[...]
        mult = 1.0 + sc_ref[...]
        shift = sh_ref[...]

        def body(r, carry):
            rows = pl.ds(pl.multiple_of(r * row_chunk, row_chunk), row_chunk)
            x = x_ref[rows, :]
            if has_pos:
                x = x + pos_ref[rows, :]
                h_ref[rows, :] = x
            ms = jnp.mean(x * x, axis=-1, keepdims=True)
            y = x * lax.rsqrt(ms + EPS) * gain
            xn_ref[rows, :] = (y * mult + shift).astype(BF16)
            return carry

        lax.fori_loop(0, tm // row_chunk, body, 0)

    acc = jnp.dot(xn_ref[...], w_ref[...], preferred_element_type=F32)
    o_ref[...] = _act(acc, act).astype(o_ref.dtype)


def _nm_call(x, pos, gain, mods, layer, k_shift, mod_row, w, *, act, out_dtype, tm, tn, rows_per_seq, name):
    m, d = x.shape
    n = w.shape[1]
    has_pos = pos is not None
    tiles_per_seq = rows_per_seq // tm
    in_specs = [pl.BlockSpec((tm, d), lambda i, j: (i, 0))]
    args = [x]
    if has_pos:
        in_specs.append(pl.BlockSpec((tm, d), lambda i, j: (i % tiles_per_seq, 0)))
        args.append(pos)
    in_specs += [
        pl.BlockSpec((1, d), lambda i, j: (0, 0)),
        pl.BlockSpec((None, None, None, 1, d), lambda i, j: (layer, mod_row(i), k_shift, 0, 0)),
        pl.BlockSpec((None, None, None, 1, d), lambda i, j: (layer, mod_row(i), k_shift + 1, 0, 0)),
        pl.BlockSpec((d, tn), lambda i, j: (0, j)),
    ]
    args += [gain.reshape(1, d), mods, mods, w]
    out_specs = [pl.BlockSpec((tm, tn), lambda i, j: (i, j))]
    out_shape = [jax.ShapeDtypeStruct((m, n), out_dtype)]
    if has_pos:
        out_specs.append(pl.BlockSpec((tm, d), lambda i, j: (i, 0)))
        out_shape.append(jax.ShapeDtypeStruct((m, d), F32))
    res = pl.pallas_call(
        functools.partial(_nm_kernel, has_pos=has_pos, act=act, row_chunk=64),
        grid=(m // tm, n // tn),
        in_specs=in_specs,
        out_specs=out_specs,
        out_shape=out_shape,
        scratch_shapes=[pltpu.VMEM((tm, d), BF16)],
        compiler_params=_params("arbitrary", "arbitrary"),
        name=name,
    )(*args)
    return res if has_pos else res[0]


def _mr_kernel(*refs, has_final):
    if has_final:
        a_ref, w_ref, h_ref, gate_ref, fg_ref, o_ref, acc_ref = refs
    else:
        a_ref, w_ref, h_ref, gate_ref, o_ref, acc_ref = refs
    k = pl.program_id(1)

    @pl.when(k == 0)
    def _():
        acc_ref[...] = jnp.zeros_like(acc_ref)

    acc_ref[...] += jnp.dot(a_ref[...], w_ref[...], preferred_element_type=F32)

    @pl.when(k == pl.num_programs(1) - 1)
    def _():
        y = h_ref[...] + gate_ref[...] * acc_ref[...]
        if has_final:
            ms = jnp.mean(y * y, axis=-1, keepdims=True)
            y = y * lax.rsqrt(ms + EPS) * fg_ref[...]
        o_ref[...] = y


def _mr_call(a, w, h, mods, layer, k_gate, final_g, *, tm, tk, rows_per_seq, name):
    m, kdim = a.shape
    d = w.shape[1]
    has_final = final_g is not None
    in_specs = [
        pl.BlockSpec((tm, tk), lambda i, k: (i, k)),
        pl.BlockSpec((tk, d), lambda i, k: (k, 0)),
        pl.BlockSpec((tm, d), lambda i, k: (i, 0)),
        pl.BlockSpec((None, None, None, 1, d), lambda i, k: (layer, (i * tm) // rows_per_seq, k_gate, 0, 0)),
    ]
    args = [a, w, h, mods]
    if has_final:
        in_specs.append(pl.BlockSpec((1, d), lambda i, k: (0, 0)))
        args.append(final_g.reshape(1, d))
    return pl.pallas_call(
        functools.partial(_mr_kernel, has_final=has_final),
        grid=(m // tm, kdim // tk),
        in_specs=in_specs,
        out_specs=pl.BlockSpec((tm, d), lambda i, k: (i, 0)),
        out_shape=jax.ShapeDtypeStruct((m, d), F32),
        scratch_shapes=[pltpu.VMEM((tm, d), F32)],
        compiler_params=_params("arbitrary", "arbitrary"),
        name=name,
    )(*args)


def _shift_down(x, s, row):
    return jnp.where(row >= s, pltpu.roll(x, s, 0), 0.0)


def _shift_up(x, s, row):
    n = x.shape[0]
    return jnp.where(row < n - s, pltpu.roll(x, n - s, 0), 0.0)


def _gates_kernel(ab_ref, alog_ref, dtb_ref, o_ref, *, heads):
    x = ab_ref[...]
    n = x.shape[0]
    lane = lax.broadcasted_iota(jnp.int32, x.shape, 1)
    row = lax.broadcasted_iota(jnp.int32, x.shape, 0)
    pos = row % GDN_CHUNK
    g = -jnp.exp(alog_ref[...]) * jax.nn.softplus(x + dtb_ref[...])
    pre = g
    suf = g
    s = 1
    while s < GDN_CHUNK:
        pre = pre + jnp.where(pos >= s, pltpu.roll(pre, s, 0), 0.0)
        suf = suf + jnp.where(pos < GDN_CHUNK - s, pltpu.roll(suf, n - s, 0), 0.0)
        s *= 2
    beta = jax.nn.sigmoid(x)
    o_ref[...] = jnp.where(lane < heads, pre, jnp.where(lane < 2 * heads, suf, beta))


def _gates_call(p, col_block, alog_row, dtb_row, heads):
    b, n, _ = p.shape
    return pl.pallas_call(
        functools.partial(_gates_kernel, heads=heads),
        grid=(b,),
        in_specs=[
            pl.BlockSpec((None, n, LANES), lambda i: (i, 0, col_block)),
            pl.BlockSpec((1, LANES), lambda i: (0, 0)),
            pl.BlockSpec((1, LANES), lambda i: (0, 0)),
        ],
        out_specs=pl.BlockSpec((None, n, LANES), lambda i: (i, 0, 0)),
        out_shape=jax.ShapeDtypeStruct((b, n, LANES), F32),
        compiler_params=_params("arbitrary"),
        name="gdn_gates",
    )(p, alog_row, dtb_row)


def _conv_silu(x, w_ref, row):
    y = x * w_ref[2:3, :]
    y = y + _shift_down(x, 2, row) * w_ref[0:1, :]
    y = y + _shift_down(x, 1, row) * w_ref[1:2, :]
    y = y + _shift_up(x, 1, row) * w_ref[3:4, :]
    y = y + _shift_up(x, 2, row) * w_ref[4:5, :]
    return _silu(y)


def _l2norm(t):
    return t * lax.rsqrt(jnp.sum(t * t, axis=-1, keepdims=True) + EPS)


def _dot_nt(a, b):
    return lax.dot_general(a, b, (((1,), (1,)), ((), ())), preferred_element_type=F32)


def _dot_tn(a, b):
    return lax.dot_general(a, b, (((0,), (0,)), ((), ())), preferred_element_type=F32)


def _dot(a, b):
    return jnp.dot(a, b, preferred_element_type=F32)


def _unit_triangular_inverse(a, backward):
    c = a.shape[0]
    ri = lax.broadcasted_iota(jnp.int32, (c, c), 0)
    ci = lax.broadcasted_iota(jnp.int32, (c, c), 1)
    lo, hi = (ri, ci) if backward else (ci, ri)
    eye = (ri == ci).astype(F32)
    t = eye - jnp.where((ri ^ ci) == 1, a, 0.0)
    s = 2
    while s < c:
        couple = ((ri ^ ci) < 2 * s) & ((hi & s) != 0) & ((lo & s) == 0)
        a_s = jnp.where(couple, a, 0.0).astype(BF16)
        tb = t.astype(BF16)
        t = t - _dot(tb, _dot(a_s, tb).astype(BF16))
        s *= 2
    return t


def _gdn_chunk(state, q, k, v, gam, beta, gam_row, backward, with_output):
    c = q.shape[0]
    ri = lax.broadcasted_iota(jnp.int32, (c, c), 0)
    ci = lax.broadcasted_iota(jnp.int32, (c, c), 1)
    incl = (ri <= ci) if backward else (ri >= ci)
    strict = (ri < ci) if backward else (ri > ci)
    diff = gam[:, :c] - gam_row
    dmat = jnp.where(incl, jnp.exp(jnp.where(incl, diff, 0.0)), 0.0)
    kb = k * beta
    kbf = k.astype(BF16)
    a_mat = jnp.where(strict, _dot_nt(kb.astype(BF16), kbf) * dmat, 0.0)
    t_inv = _unit_triangular_inverse(a_mat, backward).astype(BF16)
    egam = jnp.exp(gam)
    u = _dot(t_inv, (v * beta).astype(BF16))
    w = _dot(t_inv, (kb * egam).astype(BF16))
    g_last = gam[0:1, :] if backward else gam[c - 1:c, :]
    k_dec = k * jnp.exp(g_last - gam)
    sb = state.astype(BF16)
    v_new = u - _dot(w.astype(BF16), sb)
    vnb = v_new.astype(BF16)
    out = None
    if with_output:
        qk = jnp.where(incl, _dot_nt(q.astype(BF16), kbf) * dmat, 0.0)
        out = _dot((q * egam).astype(BF16), sb) + _dot(qk.astype(BF16), vnb)
    new_state = state * jnp.exp(g_last[:, 0:1]) + _dot_tn(k_dec.astype(BF16), vnb)
    return new_state, out


def _gdn_kernel(qx_ref, kx_ref, vx_ref, zx_ref, qc_ref, kc_ref, vc_ref,
                gx_ref, grx_ref, gc_ref, grc_ref, wq_ref, wk_ref, wv_ref, og_ref,
                y_ref,
                q_s, k_s, v_s, qc_s, kc_s, vc_s, gam_s, beta_s, gamc_s, betac_s, of_s, ob_s, *, heads):
    h = pl.program_id(1)
    n_x = qx_ref.shape[0]
    n_c = qc_ref.shape[0]
    c = GDN_CHUNK

    for (src, dst, wref, kind), n in [((qx_ref, q_s, wq_ref, "q"), n_x), ((kx_ref, k_s, wk_ref, "k"), n_x),
                                      ((vx_ref, v_s, wv_ref, "v"), n_x), ((qc_ref, qc_s, wq_ref, "q"), n_c),
                                      ((kc_ref, kc_s, wk_ref, "k"), n_c), ((vc_ref, vc_s, wv_ref, "v"), n_c)]:
        row = lax.broadcasted_iota(jnp.int32, (n, LANES), 0)
        t = _conv_silu(src[...], wref, row)
        if kind == "q":
            t = _l2norm(t) * (HEAD_DIM ** -0.5)
        elif kind == "k":
            t = _l2norm(t)
        dst[...] = t

    for g_ref, gam_dst, beta_dst in ((gx_ref, gam_s, beta_s), (gc_ref, gamc_s, betac_s)):
        blk = g_ref[...]
        lane = lax.broadcasted_iota(jnp.int32, blk.shape, 1)
        for dirn in range(2):
            gcol = jnp.sum(jnp.where(lane == dirn * heads + h, blk, 0.0), axis=-1, keepdims=True)
            bcol = jnp.sum(jnp.where(lane == (2 + dirn) * heads + h, blk, 0.0), axis=-1, keepdims=True)
            gam_dst[dirn] = jnp.broadcast_to(gcol, blk.shape)
            beta_dst[dirn] = jnp.broadcast_to(bcol, blk.shape)

    def run(n_rows, qs, ks, vs, gams, betas, grow_ref, states, with_output):
        n_chunks = n_rows // c

        def body(i, carry):
            sf, sb = carry
            new = []
            for dirn, st in ((0, sf), (1, sb)):
                ch = i if dirn == 0 else n_chunks - 1 - i
                rows = pl.ds(pl.multiple_of(ch * c, c), c)
                gam_row = grow_ref[dirn * heads + h, pl.ds(ch, 1), :]
                st, out = _gdn_chunk(st, qs[rows, :], ks[rows, :], vs[rows, :], gams[dirn, rows, :],
                                     betas[dirn, rows, :], gam_row, dirn == 1, with_output)
                if with_output:
                    (of_s if dirn == 0 else ob_s)[rows, :] = out
                new.append(st)
            return tuple(new)

        return lax.fori_loop(0, n_chunks, body, states)

    zero = jnp.zeros((HEAD_DIM, HEAD_DIM), F32)
    states = run(n_c, qc_s, kc_s, vc_s, gamc_s, betac_s, grc_ref, (zero, zero), False)
    run(n_x, q_s, k_s, v_s, gam_s, beta_s, grx_ref, states, True)

    o = of_s[...] + ob_s[...]
    ms = jnp.mean(o * o, axis=-1, keepdims=True)
    y = o * lax.rsqrt(ms + EPS) * og_ref[...]
    y_ref[...] = (y * _silu(zx_ref[...])).astype(y_ref.dtype)


def _gdn_call(px, pc, gx, grx, gc, grc, conv_w, onorm_g, heads, blk_q, blk_k, blk_v, blk_z):
    b, n_x, _ = px.shape
    n_c = pc.shape[1]
    c = GDN_CHUNK

    def col(blk0):
        return lambda i, h: (i, 0, blk0 + h)

    def seq_spec(n, blk0):
        return pl.BlockSpec((None, n, LANES), col(blk0))

    def wspec(blk0):
        return pl.BlockSpec((SHORT_CONV, LANES), lambda i, h: (0, blk0 + h))

    in_specs = [
        seq_spec(n_x, blk_q), seq_spec(n_x, blk_k), seq_spec(n_x, blk_v), seq_spec(n_x, blk_z),
        seq_spec(n_c, blk_q), seq_spec(n_c, blk_k), seq_spec(n_c, blk_v),
        pl.BlockSpec((None, n_x, LANES), lambda i, h: (i, 0, 0)),
        pl.BlockSpec((None, 2 * heads, n_x // c, c), lambda i, h: (i, 0, 0, 0)),
        pl.BlockSpec((None, n_c, LANES), lambda i, h: (i, 0, 0)),
        pl.BlockSpec((None, 2 * heads, n_c // c, c), lambda i, h: (i, 0, 0, 0)),
        wspec(0), wspec(heads), wspec(2 * heads),
        pl.BlockSpec((1, LANES), lambda i, h: (0, 0)),
    ]
    seq = lambda n: pltpu.VMEM((n, LANES), F32)
    seq2 = lambda n: pltpu.VMEM((2, n, LANES), F32)
    return pl.pallas_call(
        functools.partial(_gdn_kernel, heads=heads),
        grid=(b, heads),
        in_specs=in_specs,
        out_specs=pl.BlockSpec((None, n_x, LANES), lambda i, h: (i, 0, h)),
        out_shape=jax.ShapeDtypeStruct((b, n_x, heads * HEAD_DIM), BF16),
        scratch_shapes=[seq(n_x), seq(n_x), seq(n_x), seq(n_c), seq(n_c), seq(n_c),
                        seq2(n_x), seq2(n_x), seq2(n_c), seq2(n_c), seq(n_x), seq(n_x)],
        compiler_params=_params("arbitrary", "arbitrary"),
        name="gdn",
    )(px, px, px, px, pc, pc, pc, gx, grx, gc, grc, conv_w, conv_w, conv_w, onorm_g.reshape(1, LANES))


def _pool_kernel(p_ref, w_ref, s_ref, o_ref):
    n = p_ref.shape[0]
    gc = w_ref.shape[1]
    row = lax.broadcasted_iota(jnp.int32, (n, gc), 0)
    t = lax.broadcasted_iota(jnp.int32, (n, 1), 0)
    for gi, r in enumerate(POOL_RADII):
        cols = slice(gi * gc, (gi + 1) * gc)
        x = p_ref[:, cols]
        tot = x
        for s in range(1, r + 1):
            tot = tot + _shift_down(x, s, row) + _shift_up(x, s, row)
        cnt = (jnp.minimum(t + r + 1, n) - jnp.maximum(t - r, 0)).astype(F32)
        pooled = tot / cnt - x
        y = jnp.dot(pooled.astype(BF16), w_ref[gi], preferred_element_type=F32)
        o_ref[:, cols] = (y * s_ref[:, cols]).astype(o_ref.dtype)


def _pool_call(px, col_block, pool_w, pool_scale):
    b, n, _ = px.shape
    g, gc, _ = pool_w.shape
    width = g * gc
    return pl.pallas_call(
        _pool_kernel,
        grid=(b,),
        in_specs=[
            pl.BlockSpec((None, n, width), lambda i: (i, 0, col_block)),
            pl.BlockSpec((g, gc, gc), lambda i: (0, 0, 0)),
            pl.BlockSpec((1, width), lambda i: (0, 0)),
        ],
        out_specs=pl.BlockSpec((None, n, width), lambda i: (i, 0, 0)),
        out_shape=jax.ShapeDtypeStruct((b, n, width), BF16),
        compiler_params=_params("arbitrary"),
        name="pool",
    )(px, pool_w, pool_scale.reshape(1, width))


def _sgu_kernel(u_ref, v_ref, lg_ref, lb_ref, ws_ref, bs_ref, o_ref, vn_ref):
    tr, wdt = v_ref.shape
    gw = wdt // SGU_GROUPS
    rc = 64

    def ln_body(r, carry):
        rows = pl.ds(pl.multiple_of(r * rc, rc), rc)
        v = v_ref[rows, :]
        mu = jnp.mean(v, axis=-1, keepdims=True)
        d = v - mu
        var = jnp.mean(d * d, axis=-1, keepdims=True)
        vn_ref[rows, :] = (d * lax.rsqrt(var + EPS) * lg_ref[...] + lb_ref[...]).astype(BF16)
        return carry

    lax.fori_loop(0, tr // rc, ln_body, 0)
    for ch in range(tr // SGU_CHUNK):
        rows = slice(ch * SGU_CHUNK, (ch + 1) * SGU_CHUNK)
        for g in range(SGU_GROUPS):
            cols = slice(g * gw, (g + 1) * gw)
            mixed = jnp.dot(ws_ref[g], vn_ref[rows, cols], preferred_element_type=F32) + bs_ref[:, g:g + 1]
            o_ref[rows, cols] = (u_ref[rows, cols] * mixed).astype(o_ref.dtype)


def _sgu_call(zz, ln_g, ln_b, ws, bs_t, tr):
    b, n, w2 = zz.shape
    wdt = w2 // 2
    return pl.pallas_call(
        _sgu_kernel,
        grid=(b, n // tr),
        in_specs=[
            pl.BlockSpec((None, tr, wdt), lambda i, r: (i, r, 0)),
            pl.BlockSpec((None, tr, wdt), lambda i, r: (i, r, 1)),
            pl.BlockSpec((1, wdt), lambda i, r: (0, 0)),
            pl.BlockSpec((1, wdt), lambda i, r: (0, 0)),
            pl.BlockSpec((SGU_GROUPS, SGU_CHUNK, SGU_CHUNK), lambda i, r: (0, 0, 0)),
            pl.BlockSpec((SGU_CHUNK, SGU_GROUPS), lambda i, r: (0, 0)),
        ],
        out_specs=pl.BlockSpec((None, tr, wdt), lambda i, r: (i, r, 0)),
        out_shape=jax.ShapeDtypeStruct((b, n, wdt), BF16),
        scratch_shapes=[pltpu.VMEM((tr, wdt), BF16)],
        compiler_params=_params("arbitrary", "arbitrary"),
        name="sgu",
    )(zz, zz, ln_g.reshape(1, wdt), ln_b.reshape(1, wdt), ws, bs_t)


def _grid_pos_embed(n, d):
    rows = n // GRID_W
    nf = d // 4
    omega = 1.0 / (10000.0 ** (jnp.arange(nf, dtype=F32) / nf))
    r = jnp.repeat(jnp.arange(rows, dtype=F32), GRID_W)
    col = jnp.tile(jnp.arange(GRID_W, dtype=F32), rows)
    ar = r[:, None] * omega
    ac = col[:, None] * omega
    return jnp.concatenate([jnp.sin(ar), jnp.cos(ar), jnp.sin(ac), jnp.cos(ac)], axis=-1)


def _pick_tile(n, candidates):
    for t in candidates:
        if n % t == 0:
            return t
    raise ValueError(f"no tile for {n}")


def _mlp(h, mods, layer, norm_g, w1, w2, final_g, n_seq, batch_row):
    m = h.shape[0]
    tm = _pick_tile(n_seq, (1024, 512, 256, 128))
    hid = _nm_call(h, None, norm_g, mods, layer, 3, batch_row(tm), w1, act="relu2", out_dtype=BF16,
                   tm=tm, tn=_pick_tile(w1.shape[1], (1024, 512, 256, 128)), rows_per_seq=n_seq, name=f"mlp_up{layer}")
    tm2 = _pick_tile(n_seq, (512, 256, 128))
    return _mr_call(hid, w2, h, mods, layer, 5, final_g, tm=tm2, tk=_pick_tile(w2.shape[0], (1024, 512, 256, 128)),
                    rows_per_seq=n_seq, name=f"mlp_down{layer}")


def kernel(x, c, ctx, c_ctx, ada_w, ada_b, norm1_g, norm2_g, mlp_w1, mlp_w2, ev_w_in, ev_conv_w, ev_a_log,
           ev_dt_bias, ev_onorm_g, ev_pool_w, ev_pool_scale, ev_w_out, od_w_in, od_ln_g, od_ln_b, od_ws, od_bs,
           od_w_out, final_g):
    b, n_seq, d = x.shape
    n_ctx = ctx.shape[1]
    depth = ada_w.shape[0]
    heads = ev_a_log.shape[-1]
    qk_w = heads * HEAD_DIM
    gdn_qkv = 3 * qk_w
    gdn_in = gdn_qkv + 4 * heads
    pool_w_dim = ev_pool_scale.shape[-1]
    assert b + 1 <= MOD_ROWS and 4 * heads <= LANES and n_seq % GDN_CHUNK == 0 and n_ctx % GDN_CHUNK == 0
    assert qk_w % 1024 == 0 and pool_w_dim == qk_w

    cc = jnp.zeros((MOD_ROWS, d), F32).at[:b].set(c).at[b].set(c_ctx)
    mods = _ada_call(cc, ada_w, ada_b).reshape(depth, MOD_ROWS, N_ADA, 1, d)

    def batch_row(tm):
        return lambda i: (i * tm) // n_seq

    pos = _grid_pos_embed(n_seq, d)
    h = x.reshape(b * n_seq, d)
    last_even = 2 * ((depth - 1) // 2)
    assert last_even == 0, "context stream is only advanced through its first DeltaNet layer"

    for i in range(depth):
        j = i // 2
        fin = final_g if i == depth - 1 else None
        if i % 2 == 0:
            w_in = ev_w_in[j]
            ab_pad = LANES - 4 * heads
            n_cols = 5 * qk_w + LANES
            n_pad = -n_cols % 768
            w_x = jnp.concatenate([
                w_in[:, :gdn_qkv], w_in[:, gdn_in:gdn_in + 2 * qk_w], w_in[:, gdn_qkv:gdn_in],
                jnp.zeros((d, ab_pad + n_pad), F32)], axis=1).astype(BF16)
            w_c = jnp.concatenate([w_in[:, :gdn_in], jnp.zeros((d, ab_pad), F32)], axis=1).astype(BF16)
            blk = qk_w // LANES
            tm = _pick_tile(n_seq, (512, 256, 128) if i == 0 else (1024, 512, 256, 128))
            if i == 0:
                px, h = _nm_call(h, pos, norm1_g[i], mods, i, 0, batch_row(tm), w_x, act="none", out_dtype=F32,
                                 tm=tm, tn=768, rows_per_seq=n_seq, name="even_in")
            else:
                px = _nm_call(h, None, norm1_g[i], mods, i, 0, batch_row(tm), w_x, act="none", out_dtype=F32,
                              tm=tm, tn=768, rows_per_seq=n_seq, name="even_in")
            tmc = _pick_tile(n_ctx, (1024, 512, 256, 128))
            pc = _nm_call(ctx.reshape(b * n_ctx, d), None, norm1_g[i], mods, i, 0, lambda t: b, w_c, act="none",
                          out_dtype=F32, tm=tmc, tn=_pick_tile(w_c.shape[1], (640, 128)), rows_per_seq=n_ctx,
                          name="even_in_ctx")
            px = px.reshape(b, n_seq, -1)
            pc = pc.reshape(b, n_ctx, -1)

            lane_pad = jnp.zeros((LANES - 2 * heads,), F32)
            alog_row = jnp.concatenate([ev_a_log[j].reshape(-1), lane_pad]).reshape(1, LANES)
            dtb_row = jnp.concatenate([ev_dt_bias[j].reshape(-1), lane_pad]).reshape(1, LANES)
            gx = _gates_call(px, 5 * blk, alog_row, dtb_row, heads)
            gc = _gates_call(pc, 3 * blk, alog_row, dtb_row, heads)

            def rows_of(g, n):
                return jnp.transpose(g[:, :, :2 * heads].reshape(b, n // GDN_CHUNK, GDN_CHUNK, 2 * heads), (0, 3, 1, 2))

            y_a = _gdn_call(px, pc, gx, rows_of(gx, n_seq), gc, rows_of(gc, n_ctx), ev_conv_w[j], ev_onorm_g[j],
                            heads, 0, blk, 2 * blk, 3 * blk)
            y_b = _pool_call(px, 4 * qk_w // pool_w_dim, ev_pool_w[j].astype(BF16), ev_pool_scale[j])
            y = jnp.concatenate([y_a, y_b], axis=-1).reshape(b * n_seq, -1)
            w_out = ev_w_out[j].astype(BF16)
        else:
            tm = _pick_tile(n_seq, (1024, 512, 256, 128))
            w_od = od_w_in[j].astype(BF16)
            zz = _nm_call(h, None, norm1_g[i], mods, i, 0, batch_row(tm), w_od, act="gelu", out_dtype=F32,
                          tm=tm, tn=_pick_tile(w_od.shape[1], (1024, 512, 256, 128)), rows_per_seq=n_seq,
                          name="odd_in")
            y = _sgu_call(zz.reshape(b, n_seq, -1), od_ln_g[j], od_ln_b[j], od_ws[j].astype(BF16),
                          jnp.transpose(od_bs[j]), _pick_tile(n_seq, (512, 256, 128)))
            y = y.reshape(b * n_seq, -1)
            w_out = od_w_out[j].astype(BF16)
        tm2 = _pick_tile(n_seq, (512, 256, 128))
        h = _mr_call(y, w_out, h, mods, i, 2, None, tm=tm2, tk=_pick_tile(y.shape[1], (1024, 512, 256, 128)),
                     rows_per_seq=n_seq, name=f"mix_out{i}")
        h = _mlp(h, mods, i, norm2_g[i], mlp_w1[i].astype(BF16), mlp_w2[i].astype(BF16), fin, n_seq, batch_row)
    return h.reshape(b, n_seq, d)
```

```python
import functools
import math

import jax
import jax.numpy as jnp
import numpy as np
from jax import lax
from jax.experimental import pallas as pl
from jax.experimental.pallas import tpu as pltpu

F32 = jnp.float32
BF16 = jnp.bfloat16

EPS = 1e-6
GRID_W = 64
N_ADA = 6
HEAD_DIM = 128
GDN_CHUNK = 64
SHORT_CONV = 5
POOL_RADII = (1, 2, 4, 8)
SGU_GROUPS = 4
SGU_CHUNK = 128

LANES = 128
SUBLANES = 8
MOD_ROWS = 32
VMEM_LIMIT = 56 * 1024 * 1024

GDN_HEADS_PER_STEP = 2
GDN_CHUNK_GROUP = 6


def _params(*sem):
    return pltpu.CompilerParams(dimension_semantics=sem, vmem_limit_bytes=VMEM_LIMIT)


def _silu(x):
    return x * jax.nn.sigmoid(x)


def _ada_kernel(c_ref, w_ref, b_ref, o_ref):
    s = _silu(c_ref[...]).astype(BF16)
    o_ref[...] = jnp.dot(s, w_ref[...].astype(BF16), preferred_element_type=F32) + b_ref[...]


def _ada_call(cc, ada_w, ada_b):
    depth, d, n = ada_w.shape
    tn = 1024
    return pl.pallas_call(
        _ada_kernel,
        grid=(depth, n // tn),
        in_specs=[
            pl.BlockSpec((MOD_ROWS, d), lambda l, j: (0, 0)),
            pl.BlockSpec((None, d, tn), lambda l, j: (l, 0, j)),
            pl.BlockSpec((None, 1, tn), lambda l, j: (l, 0, j)),
        ],
        out_specs=pl.BlockSpec((None, MOD_ROWS, tn), lambda l, j: (l, 0, j)),
        out_shape=jax.ShapeDtypeStruct((depth, MOD_ROWS, n), F32),
        compiler_params=_params("arbitrary", "arbitrary"),
        name="ada",
    )(cc, ada_w, ada_b.reshape(depth, 1, n))


def _act(y, act):
    if act == "relu2":
        r = jnp.maximum(y, 0.0)
        return r * r
    if act == "gelu":
        return 0.5 * y * (1.0 + lax.erf(y * math.sqrt(0.5)))
    return y


def _nm_kernel(*refs, has_pos, act, row_chunk):
    if has_pos:
        x_ref, pos_ref, g_ref, sh_ref, sc_ref, w_ref, o_ref, h_ref, xn_ref = refs
    else:
        x_ref, g_ref, sh_ref, sc_ref, w_ref, o_ref, xn_ref = refs
    tm = x_ref.shape[0]

    @pl.when(pl.program_id(1) == 0)
    def _():
        gain = g_ref[...]
        mult = 1.0 + sc_ref[...]
        shift = sh_ref[...]

        def body(r, carry):
            rows = pl.ds(pl.multiple_of(r * row_chunk, row_chunk), row_chunk)
            x = x_ref[rows, :]
            if has_pos:
                x = x + pos_ref[rows, :]
                h_ref[rows, :] = x
            ms = jnp.mean(x * x, axis=-1, keepdims=True)
            y = x * lax.rsqrt(ms + EPS) * gain
            xn_ref[rows, :] = (y * mult + shift).astype(BF16)
            return carry

        lax.fori_loop(0, tm // row_chunk, body, 0, unroll=8)

    acc = jnp.dot(xn_ref[...], w_ref[...], preferred_element_type=F32)
    o_ref[...] = _act(acc, act).astype(o_ref.dtype)


def _nm_call(x, pos, gain, mods, layer, k_shift, mod_row, w, *, act, out_dtype, tm, tn, rows_per_seq, name):
    m, d = x.shape
    n = w.shape[1]
    has_pos = pos is not None
    tiles_per_seq = rows_per_seq // tm
    in_specs = [pl.BlockSpec((tm, d), lambda i, j: (i, 0))]
    args = [x]
    if has_pos:
        in_specs.append(pl.BlockSpec((tm, d), lambda i, j: (i % tiles_per_seq, 0)))
        args.append(pos)
    in_specs += [
        pl.BlockSpec((1, d), lambda i, j: (0, 0)),
        pl.BlockSpec((None, None, None, 1, d), lambda i, j: (layer, mod_row(i), k_shift, 0, 0)),
        pl.BlockSpec((None, None, None, 1, d), lambda i, j: (layer, mod_row(i), k_shift + 1, 0, 0)),
        pl.BlockSpec((d, tn), lambda i, j: (0, j)),
    ]
    args += [gain.reshape(1, d), mods, mods, w]
    out_specs = [pl.BlockSpec((tm, tn), lambda i, j: (i, j))]
    out_shape = [jax.ShapeDtypeStruct((m, n), out_dtype)]
    if has_pos:
        out_specs.append(pl.BlockSpec((tm, d), lambda i, j: (i, 0)))
        out_shape.append(jax.ShapeDtypeStruct((m, d), F32))
    res = pl.pallas_call(
        functools.partial(_nm_kernel, has_pos=has_pos, act=act, row_chunk=16),
        grid=(m // tm, n // tn),
        in_specs=in_specs,
        out_specs=out_specs,
        out_shape=out_shape,
        scratch_shapes=[pltpu.VMEM((tm, d), BF16)],
        compiler_params=_params("arbitrary", "arbitrary"),
        name=name,
    )(*args)
    return res if has_pos else res[0]


def _mr_kernel(*refs, has_final, k_steps):
    if has_final:
        a_ref, w_ref, h_ref, gate_ref, fg_ref, o_ref, *acc = refs
    else:
        a_ref, w_ref, h_ref, gate_ref, o_ref, *acc = refs

    def finish(total):
        y = h_ref[...] + gate_ref[...] * total
        if has_final:
            ms = jnp.mean(y * y, axis=-1, keepdims=True)
            y = y * lax.rsqrt(ms + EPS) * fg_ref[...]
        o_ref[...] = y

    if k_steps == 1:
        finish(jnp.dot(a_ref[...], w_ref[...], preferred_element_type=F32))
        return
    acc_ref, = acc
    k = pl.program_id(1)

    @pl.when(k == 0)
    def _():
        acc_ref[...] = jnp.zeros_like(acc_ref)

    acc_ref[...] += jnp.dot(a_ref[...], w_ref[...], preferred_element_type=F32)

    @pl.when(k == k_steps - 1)
    def _():
        finish(acc_ref[...])


def _mr_call(a, w, h, mods, layer, k_gate, final_g, *, tm, tk, rows_per_seq, name):
    m, kdim = a.shape
    d = w.shape[1]
    has_final = final_g is not None
    in_specs = [
        pl.BlockSpec((tm, tk), lambda i, k: (i, k)),
        pl.BlockSpec((tk, d), lambda i, k: (k, 0)),
        pl.BlockSpec((tm, d), lambda i, k: (i, 0)),
        pl.BlockSpec((None, None, None, 1, d), lambda i, k: (layer, (i * tm) // rows_per_seq, k_gate, 0, 0)),
    ]
    args = [a, w, h, mods]
    if has_final:
        in_specs.append(pl.BlockSpec((1, d), lambda i, k: (0, 0)))
        args.append(final_g.reshape(1, d))
    k_steps = kdim // tk
    return pl.pallas_call(
        functools.partial(_mr_kernel, has_final=has_final, k_steps=k_steps),
        grid=(m // tm, k_steps),
        in_specs=in_specs,
        out_specs=pl.BlockSpec((tm, d), lambda i, k: (i, 0)),
        out_shape=jax.ShapeDtypeStruct((m, d), F32),
        scratch_shapes=[pltpu.VMEM((tm, d), F32)] if k_steps > 1 else [],
        compiler_params=_params("arbitrary", "arbitrary"),
        name=name,
    )(*args)


def _shift_down(x, s, row):
    return jnp.where(row >= s, pltpu.roll(x, s, 0), 0.0)


def _shift_up(x, s, row):
    n = x.shape[0]
    return jnp.where(row < n - s, pltpu.roll(x, n - s, 0), 0.0)


def _gates_kernel(ab_ref, alog_ref, dtb_ref, o_ref, *, heads):
    x = ab_ref[...]
    n = x.shape[0]
    lane = lax.broadcasted_iota(jnp.int32, x.shape, 1)
    row = lax.broadcasted_iota(jnp.int32, x.shape, 0)
    pos = row % GDN_CHUNK
    g = -jnp.exp(alog_ref[...]) * jax.nn.softplus(x + dtb_ref[...])
    pre = g
    suf = g
    s = 1
    while s < GDN_CHUNK:
        pre = pre + jnp.where(pos >= s, pltpu.roll(pre, s, 0), 0.0)
        suf = suf + jnp.where(pos < GDN_CHUNK - s, pltpu.roll(suf, n - s, 0), 0.0)
        s *= 2
    beta = jax.nn.sigmoid(x)
    o_ref[...] = jnp.where(lane < heads, pre, jnp.where(lane < 2 * heads, suf, beta))


def _gates_call(p, col_block, alog_row, dtb_row, heads):
    b, n, _ = p.shape
    return pl.pallas_call(
        functools.partial(_gates_kernel, heads=heads),
        grid=(b,),
        in_specs=[
            pl.BlockSpec((None, n, LANES), lambda i: (i, 0, col_block)),
            pl.BlockSpec((1, LANES), lambda i: (0, 0)),
            pl.BlockSpec((1, LANES), lambda i: (0, 0)),
        ],
        out_specs=pl.BlockSpec((None, n, LANES), lambda i: (i, 0, 0)),
        out_shape=jax.ShapeDtypeStruct((b, n, LANES), F32),
        compiler_params=_params("arbitrary"),
        name="gdn_gates",
    )(p, alog_row, dtb_row)


def _conv_silu(x, w, row):
    y = x * w[2:3, :]
    y = y + _shift_down(x, 2, row) * w[0:1, :]
    y = y + _shift_down(x, 1, row) * w[1:2, :]
    y = y + _shift_up(x, 1, row) * w[3:4, :]
    y = y + _shift_up(x, 2, row) * w[4:5, :]
    return _silu(y)


def _l2norm(t):
    return t * lax.rsqrt(jnp.sum(t * t, axis=-1, keepdims=True) + EPS)


def _dot_nt(a, b):
    return lax.dot_general(a, b, (((1,), (1,)), ((), ())), preferred_element_type=F32)


def _dot_tn(a, b):
    return lax.dot_general(a, b, (((0,), (0,)), ((), ())), preferred_element_type=F32)


def _dot(a, b):
    return jnp.dot(a, b, preferred_element_type=F32)


def _tri_masks():
    c = GDN_CHUNK
    r = np.arange(c)[:, None]
    col = np.arange(2 * c)[None, :]
    j = col % c
    bwd = col >= c
    hi = np.where(bwd, j, r)
    lo = np.where(bwd, r, j)
    masks = [hi >= lo, hi > lo, r == j]
    s = 1
    while s < c:
        masks.append(((r ^ j) < 2 * s) & ((hi & s) != 0) & ((lo & s) == 0))
        s *= 2
    return np.stack(masks).astype(np.float32)


def _block_diag(x):
    c = x.shape[0]
    lane = lax.broadcasted_iota(jnp.int32, x.shape, 1)
    top = jnp.where(lane < c, x, 0.0).astype(BF16)
    bot = jnp.where(lane >= c, x, 0.0).astype(BF16)
    return jnp.concatenate([top, bot], axis=0)


def _twice_diag(x16):
    z = jnp.zeros_like(x16)
    return jnp.concatenate([jnp.concatenate([x16, z], axis=1), jnp.concatenate([z, x16], axis=1)], axis=0)


def _gdn_kernel(qx_ref, kx_ref, vx_ref, zx_ref, qc_ref, kc_ref, vc_ref, gcol_ref, grow_ref, mask_ref,
                wq_ref, wk_ref, wv_ref, og_ref, y_ref,
                q_s, k_s, v_s, u_s, w_s, qd_s, kd_s, qkm_s, dl_s, o_s, *, heads, hg, chunk_group):
    hblk = pl.program_id(1)
    n_x = qx_ref.shape[0]
    n_c = qc_ref.shape[0]
    c = GDN_CHUNK
    nc_c, nc_x = n_c // c, n_x // c
    n_chunks = nc_c + nc_x
    n_levels = mask_ref.shape[0] - 3

    for hh in range(hg):
        lanes = slice(hh * LANES, (hh + 1) * LANES)
        for src_c, src_x, dst, w_ref, kind in ((qc_ref, qx_ref, q_s, wq_ref, "q"), (kc_ref, kx_ref, k_s, wk_ref, "k"),
                                               (vc_ref, vx_ref, v_s, wv_ref, "v")):
            w = w_ref[:, lanes]
            for src, r0, n in ((src_c, 0, n_c), (src_x, n_c, n_x)):
                row = lax.broadcasted_iota(jnp.int32, (n, LANES), 0)
                t = _conv_silu(src[:, lanes], w, row)
                if kind == "q":
                    t = _l2norm(t) * (HEAD_DIM ** -0.5)
                elif kind == "k":
                    t = _l2norm(t)
                dst[hh, r0:r0 + n, :] = t

    glane = lax.broadcasted_iota(jnp.int32, (c, LANES), 1)
    left = glane < c

    def wy_body(it, carry):
        items = []
        for hh in range(hg):
            for j in range(chunk_group):
                ch = it * chunk_group + j
                rows = pl.ds(pl.multiple_of(ch * c, c), c)
                head = hblk * hg + hh
                blk = gcol_ref[rows, :]

                def col(idx, blk=blk):
                    return jnp.broadcast_to(jnp.sum(jnp.where(glane == idx, blk, 0.0), axis=-1, keepdims=True), (c, LANES))

                k = k_s[hh, rows, :]
                k16 = k.astype(BF16)
                items.append(dict(hh=hh, ch=ch, rows=rows, k=k, k16=k16, q=q_s[hh, rows, :], v=v_s[hh, rows, :],
                                  gf=col(head), gb=col(heads + head), bf=col(2 * heads + head),
                                  bb=col(3 * heads + head), rp=grow_ref[hh, ch]))
        raws = [_dot_nt(jnp.concatenate([d["k16"], d["q"].astype(BF16)], axis=0),
                        jnp.concatenate([d["k16"], d["k16"]], axis=0)) for d in items]
        for d, raw in zip(items, raws):
            gam_col = jnp.where(left, d["gf"], d["gb"])
            beta_col = jnp.where(left, d["bf"], d["bb"])
            incl = mask_ref[0]
            diff = gam_col - d["rp"][0:1, :]
            decay = jnp.exp(jnp.where(incl > 0.0, diff, 0.0)) * incl
            d["a"] = raw[:c] * decay * beta_col * mask_ref[1]
            qkm_s[d["hh"], d["ch"]] = (raw[c:] * decay).astype(BF16)
            d["t"] = mask_ref[2] - d["a"] * mask_ref[3]
        for lvl in range(1, n_levels):
            xs = [_dot((d["a"] * mask_ref[3 + lvl]).astype(BF16), _block_diag(d["t"])) for d in items]
            ys = [_dot(d["t"].astype(BF16), _block_diag(x)) for d, x in zip(items, xs)]
            for d, y in zip(items, ys):
                d["t"] = d["t"] - y
        us, ws = [], []
        for d in items:
            beta_row = d["rp"][1:2, :]
            scale_w = beta_row * jnp.exp(d["rp"][0:1, :])
            us.append(_dot((d["t"] * beta_row).astype(BF16), _twice_diag(d["v"].astype(BF16))))
            ws.append(_dot((d["t"] * scale_w).astype(BF16), _twice_diag(d["k16"])))
        for d, u, w in zip(items, us, ws):
            hh, ch, rows = d["hh"], d["ch"], d["rows"]
            for dirn, gam in ((0, d["gf"]), (1, d["gb"])):
                lanes = slice(dirn * LANES, (dirn + 1) * LANES)
                g_last = gam[c - 1:c, :] if dirn == 0 else gam[0:1, :]
                u_s[hh, dirn, rows, :] = u[:, lanes]
                w_s[hh, dirn, rows, :] = w[:, lanes].astype(BF16)
                qd_s[hh, dirn, rows, :] = (d["q"] * jnp.exp(gam)).astype(BF16)
                kd_s[hh, dirn, rows, :] = (d["k"] * jnp.exp(g_last - gam)).astype(BF16)
                dl_s[hh, dirn, ch] = jnp.broadcast_to(jnp.exp(g_last), (SUBLANES, LANES))
        return carry

    lax.fori_loop(0, n_chunks // chunk_group, wy_body, 0)

    def scan(lo, hi, mode, states):
        def body(i, states):
            chains = []
            for hh in range(hg):
                for dirn in range(2):
                    if mode == "ctx":
                        ch = i if dirn == 0 else nc_c - 1 - i
                    else:
                        ch = nc_c + (i if dirn == 0 else nc_x - 1 - i)
                    rows = pl.ds(pl.multiple_of(ch * c, c), c)
                    chains.append((hh, dirn, ch, rows))
            s16 = [s.astype(BF16) for s in states]
            if mode == "ctx":
                r1 = [_dot(w_s[hh, dirn, rows, :], s) for (hh, dirn, ch, rows), s in zip(chains, s16)]
            else:
                r1 = [_dot(jnp.concatenate([w_s[hh, dirn, rows, :], qd_s[hh, dirn, rows, :]], axis=0), s)
                      for (hh, dirn, ch, rows), s in zip(chains, s16)]
            vn16 = [(u_s[hh, dirn, rows, :] - r[:c]).astype(BF16) for (hh, dirn, ch, rows), r in zip(chains, r1)]
            upd = [_dot_tn(kd_s[hh, dirn, rows, :], vn) for (hh, dirn, ch, rows), vn in zip(chains, vn16)]
            if mode != "ctx":
                zero = jnp.zeros((c, LANES), BF16)
                intra = [_dot(qkm_s[hh, ch], jnp.concatenate([vn, zero] if dirn == 0 else [zero, vn], axis=0))
                         for (hh, dirn, ch, rows), vn in zip(chains, vn16)]
                for (hh, dirn, ch, rows), r, o in zip(chains, r1, intra):
                    orow = pl.ds(pl.multiple_of((ch - nc_c) * c, c), c)
                    if mode == "set":
                        o_s[hh, orow, :] = r[c:] + o
                    else:
                        o_s[hh, orow, :] += r[c:] + o
            return tuple(s * dl_s[hh, dirn, ch][0:1, :] + up
                         for (hh, dirn, ch, rows), s, up in zip(chains, states, upd))

        return lax.fori_loop(lo, hi, body, states)

    states = tuple(jnp.zeros((HEAD_DIM, HEAD_DIM), F32) for _ in range(2 * hg))
    states = scan(0, nc_c, "ctx", states)
    states = scan(0, nc_x // 2, "set", states)
    scan(nc_x // 2, nc_x, "add", states)

    for hh in range(hg):
        lanes = slice(hh * LANES, (hh + 1) * LANES)
        o = o_s[hh]
        ms = jnp.mean(o * o, axis=-1, keepdims=True)
        y = o * lax.rsqrt(ms + EPS) * og_ref[...]
        y_ref[:, lanes] = (y * _silu(zx_ref[:, lanes])).astype(y_ref.dtype)


def _gdn_call(px, pc, gcol, grow, conv_w, onorm_g, heads, blk_q, blk_k, blk_v, blk_z):
    b, n_x, _ = px.shape
    n_c = pc.shape[1]
    c = GDN_CHUNK
    n_tot = n_c + n_x
    n_chunks = n_tot // c
    hg = GDN_HEADS_PER_STEP if heads % GDN_HEADS_PER_STEP == 0 else 1
    chunk_group = max(g for g in range(1, GDN_CHUNK_GROUP + 1) if n_chunks % g == 0)
    assert (n_x // c) % 2 == 0 and all(blk % hg == 0 for blk in (blk_q, blk_k, blk_v, blk_z))
    wide = hg * LANES
    masks = jnp.asarray(_tri_masks())

    def seq_spec(n, blk0):
        return pl.BlockSpec((None, n, wide), lambda i, h: (i, 0, blk0 // hg + h))

    def wspec(blk0):
        return pl.BlockSpec((SHORT_CONV, wide), lambda i, h: (0, blk0 // hg + h))

    in_specs = [
        seq_spec(n_x, blk_q), seq_spec(n_x, blk_k), seq_spec(n_x, blk_v), seq_spec(n_x, blk_z),
        seq_spec(n_c, blk_q), seq_spec(n_c, blk_k), seq_spec(n_c, blk_v),
        pl.BlockSpec((None, n_tot, LANES), lambda i, h: (i, 0, 0)),
        pl.BlockSpec((None, hg, n_chunks, SUBLANES, LANES), lambda i, h: (i, h, 0, 0, 0)),
        pl.BlockSpec(masks.shape, lambda i, h: (0, 0, 0)),
        wspec(0), wspec(heads), wspec(2 * heads),
        pl.BlockSpec((1, LANES), lambda i, h: (0, 0)),
    ]
    per_dir = lambda dt: pltpu.VMEM((hg, 2, n_tot, LANES), dt)
    scratch = [pltpu.VMEM((hg, n_tot, LANES), F32)] * 3 + [
        per_dir(F32), per_dir(BF16), per_dir(BF16), per_dir(BF16),
        pltpu.VMEM((hg, n_chunks, c, 2 * c), BF16),
        pltpu.VMEM((hg, 2, n_chunks, SUBLANES, LANES), F32),
        pltpu.VMEM((hg, n_x, LANES), F32),
    ]
    return pl.pallas_call(
        functools.partial(_gdn_kernel, heads=heads, hg=hg, chunk_group=chunk_group),
        grid=(b, heads // hg),
        in_specs=in_specs,
        out_specs=pl.BlockSpec((None, n_x, wide), lambda i, h: (i, 0, h)),
        out_shape=jax.ShapeDtypeStruct((b, n_x, heads * HEAD_DIM), BF16),
        scratch_shapes=scratch,
        compiler_params=_params("arbitrary", "arbitrary"),
        name="gdn",
    )(px, px, px, px, pc, pc, pc, gcol, grow, masks, conv_w, conv_w, conv_w, onorm_g.reshape(1, LANES))


def _pool_kernel(p_ref, w_ref, s_ref, o_ref):
    n = p_ref.shape[0]
    gc = w_ref.shape[1]
    row = lax.broadcasted_iota(jnp.int32, (n, gc), 0)
    t = lax.broadcasted_iota(jnp.int32, (n, 1), 0)
    for gi, r in enumerate(POOL_RADII):
        cols = slice(gi * gc, (gi + 1) * gc)
        x = p_ref[:, cols]
        tot = x
        for s in range(1, r + 1):
            tot = tot + _shift_down(x, s, row) + _shift_up(x, s, row)
        cnt = (jnp.minimum(t + r + 1, n) - jnp.maximum(t - r, 0)).astype(F32)
        pooled = tot / cnt - x
        y = jnp.dot(pooled.astype(BF16), w_ref[gi], preferred_element_type=F32)
        o_ref[:, cols] = (y * s_ref[:, cols]).astype(o_ref.dtype)


def _pool_call(px, col_block, pool_w, pool_scale):
    b, n, _ = px.shape
    g, gc, _ = pool_w.shape
    width = g * gc
    return pl.pallas_call(
        _pool_kernel,
        grid=(b,),
        in_specs=[
            pl.BlockSpec((None, n, width), lambda i: (i, 0, col_block)),
            pl.BlockSpec((g, gc, gc), lambda i: (0, 0, 0)),
            pl.BlockSpec((1, width), lambda i: (0, 0)),
        ],
        out_specs=pl.BlockSpec((None, n, width), lambda i: (i, 0, 0)),
        out_shape=jax.ShapeDtypeStruct((b, n, width), BF16),
        compiler_params=_params("arbitrary"),
        name="pool",
    )(px, pool_w, pool_scale.reshape(1, width))


def _sgu_kernel(u_ref, v_ref, lg_ref, lb_ref, ws_ref, bs_ref, o_ref, vn_ref):
    tr, wdt = v_ref.shape
    gw = wdt // SGU_GROUPS
    rc = 64

    def ln_body(r, carry):
        rows = pl.ds(pl.multiple_of(r * rc, rc), rc)
        v = v_ref[rows, :]
        mu = jnp.mean(v, axis=-1, keepdims=True)
        d = v - mu
        var = jnp.mean(d * d, axis=-1, keepdims=True)
        vn_ref[rows, :] = (d * lax.rsqrt(var + EPS) * lg_ref[...] + lb_ref[...]).astype(BF16)
        return carry

    lax.fori_loop(0, tr // rc, ln_body, 0)
    for ch in range(tr // SGU_CHUNK):
        rows = slice(ch * SGU_CHUNK, (ch + 1) * SGU_CHUNK)
        for g in range(SGU_GROUPS):
            cols = slice(g * gw, (g + 1) * gw)
            mixed = jnp.dot(ws_ref[g], vn_ref[rows, cols], preferred_element_type=F32) + bs_ref[:, g:g + 1]
            o_ref[rows, cols] = (u_ref[rows, cols] * mixed).astype(o_ref.dtype)


def _sgu_call(zz, ln_g, ln_b, ws, bs_t, tr):
    b, n, w2 = zz.shape
    wdt = w2 // 2
    return pl.pallas_call(
        _sgu_kernel,
        grid=(b, n // tr),
        in_specs=[
            pl.BlockSpec((None, tr, wdt), lambda i, r: (i, r, 0)),
            pl.BlockSpec((None, tr, wdt), lambda i, r: (i, r, 1)),
            pl.BlockSpec((1, wdt), lambda i, r: (0, 0)),
            pl.BlockSpec((1, wdt), lambda i, r: (0, 0)),
            pl.BlockSpec((SGU_GROUPS, SGU_CHUNK, SGU_CHUNK), lambda i, r: (0, 0, 0)),
            pl.BlockSpec((SGU_CHUNK, SGU_GROUPS), lambda i, r: (0, 0)),
        ],
        out_specs=pl.BlockSpec((None, tr, wdt), lambda i, r: (i, r, 0)),
        out_shape=jax.ShapeDtypeStruct((b, n, wdt), BF16),
        scratch_shapes=[pltpu.VMEM((tr, wdt), BF16)],
        compiler_params=_params("arbitrary", "arbitrary"),
        name="sgu",
    )(zz, zz, ln_g.reshape(1, wdt), ln_b.reshape(1, wdt), ws, bs_t)


def _grid_pos_embed(n, d):
    rows = n // GRID_W
    nf = d // 4
    omega = 1.0 / (10000.0 ** (jnp.arange(nf, dtype=F32) / nf))
    r = jnp.repeat(jnp.arange(rows, dtype=F32), GRID_W)
    col = jnp.tile(jnp.arange(GRID_W, dtype=F32), rows)
    ar = r[:, None] * omega
    ac = col[:, None] * omega
    return jnp.concatenate([jnp.sin(ar), jnp.cos(ar), jnp.sin(ac), jnp.cos(ac)], axis=-1)


def _pick_tile(n, candidates):
    for t in candidates:
        if n % t == 0:
            return t
    raise ValueError(f"no tile for {n}")


def _gate_rows(gcol, heads):
    b, n, _ = gcol.shape
    c = GDN_CHUNK
    g = gcol[:, :, :4 * heads].reshape(b, n // c, c, 2, 2, heads)
    g = jnp.transpose(g, (0, 5, 1, 3, 4, 2)).reshape(b, heads, n // c, 2, 2 * c)
    return jnp.pad(g, ((0, 0), (0, 0), (0, 0), (0, SUBLANES - 2), (0, 0)))


def _mlp(h, mods, layer, norm_g, w1, w2, final_g, n_seq, batch_row):
    tm = _pick_tile(n_seq, (1024, 512, 256, 128))
    hid = _nm_call(h, None, norm_g, mods, layer, 3, batch_row(tm), w1, act="relu2", out_dtype=BF16,
                   tm=tm, tn=_pick_tile(w1.shape[1], (1024, 512, 256, 128)), rows_per_seq=n_seq, name=f"mlp_up{layer}")
    tm2 = _pick_tile(n_seq, (512, 256, 128))
    return _mr_call(hid, w2, h, mods, layer, 5, final_g, tm=tm2, tk=_pick_tile(w2.shape[0], (2048, 1024, 512, 256, 128)),
                    rows_per_seq=n_seq, name=f"mlp_down{layer}")


def kernel(x, c, ctx, c_ctx, ada_w, ada_b, norm1_g, norm2_g, mlp_w1, mlp_w2, ev_w_in, ev_conv_w, ev_a_log,
           ev_dt_bias, ev_onorm_g, ev_pool_w, ev_pool_scale, ev_w_out, od_w_in, od_ln_g, od_ln_b, od_ws, od_bs,
           od_w_out, final_g):
    b, n_seq, d = x.shape
    n_ctx = ctx.shape[1]
    depth = ada_w.shape[0]
    heads = ev_a_log.shape[-1]
    qk_w = heads * HEAD_DIM
    gdn_qkv = 3 * qk_w
    gdn_in = gdn_qkv + 4 * heads
    pool_w_dim = ev_pool_scale.shape[-1]
    assert b + 1 <= MOD_ROWS and 4 * heads <= LANES and n_seq % GDN_CHUNK == 0 and n_ctx % GDN_CHUNK == 0
    assert qk_w % 1024 == 0 and pool_w_dim == qk_w

    cc = jnp.zeros((MOD_ROWS, d), F32).at[:b].set(c).at[b].set(c_ctx)
    mods = _ada_call(cc, ada_w, ada_b).reshape(depth, MOD_ROWS, N_ADA, 1, d)

    def batch_row(tm):
        return lambda i: (i * tm) // n_seq

    pos = _grid_pos_embed(n_seq, d)
    h = x.reshape(b * n_seq, d)
    last_even = 2 * ((depth - 1) // 2)
    assert last_even == 0, "context stream is only advanced through its first DeltaNet layer"

    for i in range(depth):
        j = i // 2
        fin = final_g if i == depth - 1 else None
        if i % 2 == 0:
            w_in = ev_w_in[j]
            ab_pad = LANES - 4 * heads
            n_cols = 5 * qk_w + LANES
            n_pad = -n_cols % 768
            w_x = jnp.concatenate([
                w_in[:, :gdn_qkv], w_in[:, gdn_in:gdn_in + 2 * qk_w], w_in[:, gdn_qkv:gdn_in],
                jnp.zeros((d, ab_pad + n_pad), F32)], axis=1).astype(BF16)
            w_c = jnp.concatenate([w_in[:, :gdn_in], jnp.zeros((d, ab_pad), F32)], axis=1).astype(BF16)
            blk = qk_w // LANES
            tm = _pick_tile(n_seq, (512, 256, 128) if i == 0 else (1024, 512, 256, 128))
            if i == 0:
                px, h = _nm_call(h, pos, norm1_g[i], mods, i, 0, batch_row(tm), w_x, act="none", out_dtype=F32,
                                 tm=tm, tn=768, rows_per_seq=n_seq, name="even_in")
            else:
                px = _nm_call(h, None, norm1_g[i], mods, i, 0, batch_row(tm), w_x, act="none", out_dtype=F32,
                              tm=tm, tn=768, rows_per_seq=n_seq, name="even_in")
            tmc = _pick_tile(n_ctx, (1024, 512, 256, 128))
            pc = _nm_call(ctx.reshape(b * n_ctx, d), None, norm1_g[i], mods, i, 0, lambda t: b, w_c, act="none",
                          out_dtype=F32, tm=tmc, tn=_pick_tile(w_c.shape[1], (640, 128)), rows_per_seq=n_ctx,
                          name="even_in_ctx")
            px = px.reshape(b, n_seq, -1)
            pc = pc.reshape(b, n_ctx, -1)

            lane_pad = jnp.zeros((LANES - 2 * heads,), F32)
            alog_row = jnp.concatenate([ev_a_log[j].reshape(-1), lane_pad]).reshape(1, LANES)
            dtb_row = jnp.concatenate([ev_dt_bias[j].reshape(-1), lane_pad]).reshape(1, LANES)
            gcol = jnp.concatenate([_gates_call(pc, 3 * blk, alog_row, dtb_row, heads),
                                    _gates_call(px, 5 * blk, alog_row, dtb_row, heads)], axis=1)
            y_a = _gdn_call(px, pc, gcol, _gate_rows(gcol, heads), ev_conv_w[j], ev_onorm_g[j],
                            heads, 0, blk, 2 * blk, 3 * blk)
            y_b = _pool_call(px, 4 * qk_w // pool_w_dim, ev_pool_w[j].astype(BF16), ev_pool_scale[j])
            y = jnp.concatenate([y_a, y_b], axis=-1).reshape(b * n_seq, -1)
            w_out = ev_w_out[j].astype(BF16)
        else:
            tm = _pick_tile(n_seq, (1024, 512, 256, 128))
            w_od = od_w_in[j].astype(BF16)
            zz = _nm_call(h, None, norm1_g[i], mods, i, 0, batch_row(tm), w_od, act="gelu", out_dtype=F32,
                          tm=tm, tn=_pick_tile(w_od.shape[1], (1024, 512, 256, 128)), rows_per_seq=n_seq,
                          name="odd_in")
            y = _sgu_call(zz.reshape(b, n_seq, -1), od_ln_g[j], od_ln_b[j], od_ws[j].astype(BF16),
                          jnp.transpose(od_bs[j]), _pick_tile(n_seq, (512, 256, 128)))
            y = y.reshape(b * n_seq, -1)
            w_out = od_w_out[j].astype(BF16)
        tm2 = _pick_tile(n_seq, (512, 256, 128))
        h = _mr_call(y, w_out, h, mods, i, 2, None, tm=tm2, tk=_pick_tile(y.shape[1], (2048, 1024, 512, 256, 128)),
                     rows_per_seq=n_seq, name=f"mix_out{i}")
        h = _mlp(h, mods, i, norm2_g[i], mlp_w1[i].astype(BF16), mlp_w2[i].astype(BF16), fin, n_seq, batch_row)
    return h.reshape(b, n_seq, d)
```

```python
import functools
import math

import jax
import jax.numpy as jnp
import numpy as np
from jax import lax
from jax.experimental import pallas as pl
from jax.experimental.pallas import tpu as pltpu

F32 = jnp.float32
BF16 = jnp.bfloat16

EPS = 1e-6
GRID_W = 64
N_ADA = 6
HEAD_DIM = 128
GDN_CHUNK = 64
SHORT_CONV = 5
POOL_RADII = (1, 2, 4, 8)
SGU_GROUPS = 4
SGU_CHUNK = 128

LANES = 128
SUBLANES = 8
MOD_ROWS = 32
VMEM_LIMIT = 56 * 1024 * 1024

GDN_HEADS_PER_STEP = 2
GDN_CHUNK_GROUP = 9


def _params(*sem):
    return pltpu.CompilerParams(dimension_semantics=sem, vmem_limit_bytes=VMEM_LIMIT)


def _silu(x):
    return x * jax.nn.sigmoid(x)


def _ada_kernel(c_ref, w_ref, b_ref, o_ref):
    s = _silu(c_ref[...]).astype(BF16)
    o_ref[...] = jnp.dot(s, w_ref[...].astype(BF16), preferred_element_type=F32) + b_ref[...]


def _ada_call(cc, ada_w, ada_b):
    depth, d, n = ada_w.shape
    tn = 1024
    return pl.pallas_call(
        _ada_kernel,
        grid=(depth, n // tn),
        in_specs=[
            pl.BlockSpec((MOD_ROWS, d), lambda l, j: (0, 0)),
            pl.BlockSpec((None, d, tn), lambda l, j: (l, 0, j)),
            pl.BlockSpec((None, 1, tn), lambda l, j: (l, 0, j)),
        ],
        out_specs=pl.BlockSpec((None, MOD_ROWS, tn), lambda l, j: (l, 0, j)),
        out_shape=jax.ShapeDtypeStruct((depth, MOD_ROWS, n), F32),
        compiler_params=_params("arbitrary", "arbitrary"),
        name="ada",
    )(cc, ada_w, ada_b.reshape(depth, 1, n))


def _act(y, act):
    if act == "relu2":
        r = jnp.maximum(y, 0.0)
        return r * r
    if act == "gelu":
        return 0.5 * y * (1.0 + lax.erf(y * math.sqrt(0.5)))
    return y


def _nm_kernel(*refs, has_pos, act, row_chunk):
    if has_pos:
        x_ref, pos_ref, g_ref, sh_ref, sc_ref, w_ref, o_ref, xn_ref = refs
    else:
        x_ref, g_ref, sh_ref, sc_ref, w_ref, o_ref, xn_ref = refs
    tm = x_ref.shape[0]

    @pl.when(pl.program_id(1) == 0)
    def _():
        gain = g_ref[...]
        mult = 1.0 + sc_ref[...]
        shift = sh_ref[...]

        def body(r, carry):
            rows = pl.ds(pl.multiple_of(r * row_chunk, row_chunk), row_chunk)
            x = x_ref[rows, :]
            if has_pos:
                x = x + pos_ref[rows, :]
            ms = jnp.mean(x * x, axis=-1, keepdims=True)
            y = x * lax.rsqrt(ms + EPS) * gain
            xn_ref[rows, :] = (y * mult + shift).astype(BF16)
            return carry

        lax.fori_loop(0, tm // row_chunk, body, 0, unroll=8)

    acc = jnp.dot(xn_ref[...], w_ref[...], preferred_element_type=F32)
    o_ref[...] = _act(acc, act).astype(o_ref.dtype)


def _nm_call(x, pos, gain, mods, layer, k_shift, mod_row, w, *, act, out_dtype, tm, tn, rows_per_seq, name):
    m, d = x.shape
    n = w.shape[1]
    has_pos = pos is not None
    tiles_per_seq = rows_per_seq // tm
    in_specs = [pl.BlockSpec((tm, d), lambda i, j: (i, 0))]
    args = [x]
    if has_pos:
        in_specs.append(pl.BlockSpec((tm, d), lambda i, j: (i % tiles_per_seq, 0)))
        args.append(pos)
    in_specs += [
        pl.BlockSpec((1, d), lambda i, j: (0, 0)),
        pl.BlockSpec((None, None, None, 1, d), lambda i, j: (layer, mod_row(i), k_shift, 0, 0)),
        pl.BlockSpec((None, None, None, 1, d), lambda i, j: (layer, mod_row(i), k_shift + 1, 0, 0)),
        pl.BlockSpec((d, tn), lambda i, j: (0, j)),
    ]
    args += [gain.reshape(1, d), mods, mods, w]
    return pl.pallas_call(
        functools.partial(_nm_kernel, has_pos=has_pos, act=act, row_chunk=16),
        grid=(m // tm, n // tn),
        in_specs=in_specs,
        out_specs=pl.BlockSpec((tm, tn), lambda i, j: (i, j)),
        out_shape=jax.ShapeDtypeStruct((m, n), out_dtype),
        scratch_shapes=[pltpu.VMEM((tm, d), BF16)],
        compiler_params=_params("arbitrary", "arbitrary"),
        name=name,
    )(*args)


def _mr_kernel(*refs, has_pos, has_final, k_steps):
    a_ref, w_ref, h_ref, gate_ref = refs[:4]
    rest = list(refs[4:])
    pos_ref = rest.pop(0) if has_pos else None
    fg_ref = rest.pop(0) if has_final else None
    o_ref, *acc = rest

    def finish(total):
        h = h_ref[...]
        if has_pos:
            h = h + pos_ref[...]
        y = h + gate_ref[...] * total
        if has_final:
            ms = jnp.mean(y * y, axis=-1, keepdims=True)
            y = y * lax.rsqrt(ms + EPS) * fg_ref[...]
        o_ref[...] = y

    if k_steps == 1:
        finish(jnp.dot(a_ref[...], w_ref[...], preferred_element_type=F32))
        return
    acc_ref, = acc
    k = pl.program_id(1)

    @pl.when(k == 0)
    def _():
        acc_ref[...] = jnp.zeros_like(acc_ref)

    acc_ref[...] += jnp.dot(a_ref[...], w_ref[...], preferred_element_type=F32)

    @pl.when(k == k_steps - 1)
    def _():
        finish(acc_ref[...])


def _mr_call(a, w, h, pos, mods, layer, k_gate, final_g, *, tm, tk, rows_per_seq, name):
    m, kdim = a.shape
    d = w.shape[1]
    has_final = final_g is not None
    has_pos = pos is not None
    tiles_per_seq = rows_per_seq // tm
    in_specs = [
        pl.BlockSpec((tm, tk), lambda i, k: (i, k)),
        pl.BlockSpec((tk, d), lambda i, k: (k, 0)),
        pl.BlockSpec((tm, d), lambda i, k: (i, 0)),
        pl.BlockSpec((None, None, None, 1, d), lambda i, k: (layer, (i * tm) // rows_per_seq, k_gate, 0, 0)),
    ]
    args = [a, w, h, mods]
    if has_pos:
        in_specs.append(pl.BlockSpec((tm, d), lambda i, k: (i % tiles_per_seq, 0)))
        args.append(pos)
    if has_final:
        in_specs.append(pl.BlockSpec((1, d), lambda i, k: (0, 0)))
        args.append(final_g.reshape(1, d))
    k_steps = kdim // tk
    return pl.pallas_call(
        functools.partial(_mr_kernel, has_pos=has_pos, has_final=has_final, k_steps=k_steps),
        grid=(m // tm, k_steps),
        in_specs=in_specs,
        out_specs=pl.BlockSpec((tm, d), lambda i, k: (i, 0)),
        out_shape=jax.ShapeDtypeStruct((m, d), F32),
        scratch_shapes=[pltpu.VMEM((tm, d), F32)] if k_steps > 1 else [],
        compiler_params=_params("arbitrary", "arbitrary"),
        name=name,
    )(*args)


def _gates_kernel(ab_ref, alog_ref, dtb_ref, o_ref, *, heads):
    x = ab_ref[...]
    n = x.shape[0]
    lane = lax.broadcasted_iota(jnp.int32, x.shape, 1)
    row = lax.broadcasted_iota(jnp.int32, x.shape, 0)
    pos = row % GDN_CHUNK
    g = -jnp.exp(alog_ref[...]) * jax.nn.softplus(x + dtb_ref[...])
    pre = g
    suf = g
    s = 1
    while s < GDN_CHUNK:
        pre = pre + jnp.where(pos >= s, pltpu.roll(pre, s, 0), 0.0)
        suf = suf + jnp.where(pos < GDN_CHUNK - s, pltpu.roll(suf, n - s, 0), 0.0)
        s *= 2
    beta = jax.nn.sigmoid(x)
    o_ref[...] = jnp.where(lane < heads, pre, jnp.where(lane < 2 * heads, suf, beta))


def _gates_call(p, col_block, alog_row, dtb_row, heads):
    b, n, _ = p.shape
    return pl.pallas_call(
        functools.partial(_gates_kernel, heads=heads),
        grid=(b,),
        in_specs=[
            pl.BlockSpec((None, n, LANES), lambda i: (i, 0, col_block)),
            pl.BlockSpec((1, LANES), lambda i: (0, 0)),
            pl.BlockSpec((1, LANES), lambda i: (0, 0)),
        ],
        out_specs=pl.BlockSpec((None, n, LANES), lambda i: (i, 0, 0)),
        out_shape=jax.ShapeDtypeStruct((b, n, LANES), F32),
        compiler_params=_params("arbitrary"),
        name="gdn_gates",
    )(p, alog_row, dtb_row)


def _l2norm(t):
    return t * lax.rsqrt(jnp.sum(t * t, axis=-1, keepdims=True) + EPS)


def _dot_nt(a, b):
    return lax.dot_general(a, b, (((1,), (1,)), ((), ())), preferred_element_type=F32)


def _dot_tn(a, b):
    return lax.dot_general(a, b, (((0,), (0,)), ((), ())), preferred_element_type=F32)


def _dot(a, b):
    return jnp.dot(a, b, preferred_element_type=F32)


def _tri_masks():
    c = GDN_CHUNK
    r = np.arange(c)[:, None]
    col = np.arange(2 * c)[None, :]
    j = col % c
    bwd = col >= c
    hi = np.where(bwd, j, r)
    lo = np.where(bwd, r, j)
    masks = [hi >= lo, hi > lo, r == j]
    s = 1
    while s < c:
        masks.append(((r ^ j) < 2 * s) & ((hi & s) != 0) & ((lo & s) == 0))
        s *= 2
    return np.stack(masks).astype(np.float32)


def _block_diag(x):
    c = x.shape[0]
    lane = lax.broadcasted_iota(jnp.int32, x.shape, 1)
    top = jnp.where(lane < c, x, 0.0).astype(BF16)
    bot = jnp.where(lane >= c, x, 0.0).astype(BF16)
    return jnp.concatenate([top, bot], axis=0)


def _twice_diag(x16):
    z = jnp.zeros_like(x16)
    return jnp.concatenate([jnp.concatenate([x16, z], axis=1), jnp.concatenate([z, x16], axis=1)], axis=0)


def _gdn_kernel(qx_ref, kx_ref, vx_ref, zx_ref, qc_ref, kc_ref, vc_ref, gcol_ref, grow_ref, mask_ref,
                wq_ref, wk_ref, wv_ref, og_ref, y_ref,
                q_s, k_s, v_s, u_s, w_s, qd_s, kd_s, qkm_s, dl_s, o_s, pad_s, *, heads, hg, chunk_group):
    hblk = pl.program_id(1)
    n_x = qx_ref.shape[0]
    n_c = qc_ref.shape[0]
    c = GDN_CHUNK
    nc_c, nc_x = n_c // c, n_x // c
    n_chunks = nc_c + nc_x
    n_levels = mask_ref.shape[0] - 3

    halo = SUBLANES
    pad_s[0:halo, :] = jnp.zeros((halo, LANES), F32)
    for hh in range(hg):
        lanes = slice(hh * LANES, (hh + 1) * LANES)
        for src_c, src_x, dst, w_ref, kind in ((qc_ref, qx_ref, q_s, wq_ref, "q"), (kc_ref, kx_ref, k_s, wk_ref, "k"),
                                               (vc_ref, vx_ref, v_s, wv_ref, "v")):
            w = w_ref[:, lanes]
            for src, base, n in ((src_c, 0, n_c), (src_x, n_c, n_x)):
                rb = math.gcd(n, 256)

                def copy_rows(i, carry, src=src):
                    r0 = pl.multiple_of(i * rb, rb)
                    pad_s[pl.ds(pl.multiple_of(halo + r0, halo), rb), :] = src[pl.ds(r0, rb), lanes]
                    return carry

                def conv_rows(i, carry, base=base, w=w, kind=kind, dst=dst):
                    r0 = pl.multiple_of(i * rb, rb)
                    first = halo - (SHORT_CONV - 1) // 2 + r0
                    t = w[0:1, :] * pad_s[pl.ds(first, rb), :]
                    for tap in range(1, SHORT_CONV):
                        t = t + w[tap:tap + 1, :] * pad_s[pl.ds(first + tap, rb), :]
                    t = _silu(t)
                    if kind == "q":
                        t = _l2norm(t) * (HEAD_DIM ** -0.5)
                    elif kind == "k":
                        t = _l2norm(t)
                    dst[hh, pl.ds(pl.multiple_of(base + r0, halo), rb), :] = t
                    return carry

                lax.fori_loop(0, n // rb, copy_rows, 0)
                pad_s[halo + n:2 * halo + n, :] = jnp.zeros((halo, LANES), F32)
                lax.fori_loop(0, n // rb, conv_rows, 0, unroll=min(4, n // rb))

    glane = lax.broadcasted_iota(jnp.int32, (c, LANES), 1)
    left = glane < c

    def wy_body(it, carry):
        items = []
        for hh in range(hg):
            for j in range(chunk_group):
                ch = it * chunk_group + j
                rows = pl.ds(pl.multiple_of(ch * c, c), c)
                head = hblk * hg + hh
                blk = gcol_ref[rows, :]

                def col(idx, blk=blk):
                    return jnp.broadcast_to(jnp.sum(jnp.where(glane == idx, blk, 0.0), axis=-1, keepdims=True), (c, LANES))

                k = k_s[hh, rows, :]
                k16 = k.astype(BF16)
                items.append(dict(hh=hh, ch=ch, rows=rows, k=k, k16=k16, q=q_s[hh, rows, :], v=v_s[hh, rows, :],
                                  gf=col(head), gb=col(heads + head), bf=col(2 * heads + head),
                                  bb=col(3 * heads + head), rp=grow_ref[hh, ch]))
        raws = [_dot_nt(jnp.concatenate([d["k16"], d["q"].astype(BF16)], axis=0),
                        jnp.concatenate([d["k16"], d["k16"]], axis=0)) for d in items]
        for d, raw in zip(items, raws):
            gam_col = jnp.where(left, d["gf"], d["gb"])
            beta_col = jnp.where(left, d["bf"], d["bb"])
            incl = mask_ref[0]
            diff = gam_col - d["rp"][0:1, :]
            decay = jnp.exp(jnp.where(incl > 0.0, diff, 0.0)) * incl
            d["a"] = raw[:c] * decay * beta_col * mask_ref[1]
            qkm_s[d["hh"], d["ch"]] = (raw[c:] * decay).astype(BF16)
            d["t"] = mask_ref[2] - d["a"] * mask_ref[3]
        for lvl in range(1, n_levels):
            xs = [_dot((d["a"] * mask_ref[3 + lvl]).astype(BF16), _block_diag(d["t"])) for d in items]
            ys = [_dot(d["t"].astype(BF16), _block_diag(x)) for d, x in zip(items, xs)]
            for d, y in zip(items, ys):
                d["t"] = d["t"] - y
        us, ws = [], []
        for d in items:
            beta_row = d["rp"][1:2, :]
            scale_w = beta_row * jnp.exp(d["rp"][0:1, :])
            us.append(_dot((d["t"] * beta_row).astype(BF16), _twice_diag(d["v"].astype(BF16))))
            ws.append(_dot((d["t"] * scale_w).astype(BF16), _twice_diag(d["k16"])))
        for d, u, w in zip(items, us, ws):
            hh, ch, rows = d["hh"], d["ch"], d["rows"]
            for dirn, gam in ((0, d["gf"]), (1, d["gb"])):
                lanes = slice(dirn * LANES, (dirn + 1) * LANES)
                g_last = gam[c - 1:c, :] if dirn == 0 else gam[0:1, :]
                u_s[hh, dirn, rows, :] = u[:, lanes]
                w_s[hh, dirn, rows, :] = w[:, lanes].astype(BF16)
                qd_s[hh, dirn, rows, :] = (d["q"] * jnp.exp(gam)).astype(BF16)
                kd_s[hh, dirn, rows, :] = (d["k"] * jnp.exp(g_last - gam)).astype(BF16)
                dl_s[hh, dirn, ch] = jnp.broadcast_to(jnp.exp(g_last), (SUBLANES, LANES))
        return carry

    lax.fori_loop(0, n_chunks // chunk_group, wy_body, 0)

    def scan(lo, hi, mode, states):
        def body(i, states):
            chains = []
            for hh in range(hg):
                for dirn in range(2):
                    if mode == "ctx":
                        ch = i if dirn == 0 else nc_c - 1 - i
                    else:
                        ch = nc_c + (i if dirn == 0 else nc_x - 1 - i)
                    rows = pl.ds(pl.multiple_of(ch * c, c), c)
                    chains.append((hh, dirn, ch, rows))
            s16 = [s.astype(BF16) for s in states]
            if mode == "ctx":
                r1 = [_dot(w_s[hh, dirn, rows, :], s) for (hh, dirn, ch, rows), s in zip(chains, s16)]
            else:
                r1 = [_dot(jnp.concatenate([w_s[hh, dirn, rows, :], qd_s[hh, dirn, rows, :]], axis=0), s)
                      for (hh, dirn, ch, rows), s in zip(chains, s16)]
            vn16 = [(u_s[hh, dirn, rows, :] - r[:c]).astype(BF16) for (hh, dirn, ch, rows), r in zip(chains, r1)]
            upd = [_dot_tn(kd_s[hh, dirn, rows, :], vn) for (hh, dirn, ch, rows), vn in zip(chains, vn16)]
            if mode != "ctx":
                zero = jnp.zeros((c, LANES), BF16)
                intra = [_dot(qkm_s[hh, ch], jnp.concatenate([vn, zero] if dirn == 0 else [zero, vn], axis=0))
                         for (hh, dirn, ch, rows), vn in zip(chains, vn16)]
                for (hh, dirn, ch, rows), r, o in zip(chains, r1, intra):
                    orow = pl.ds(pl.multiple_of((ch - nc_c) * c, c), c)
                    if mode == "set":
                        o_s[hh, orow, :] = r[c:] + o
                    else:
                        o_s[hh, orow, :] += r[c:] + o
            return tuple(s * dl_s[hh, dirn, ch][0:1, :] + up
                         for (hh, dirn, ch, rows), s, up in zip(chains, states, upd))

        return lax.fori_loop(lo, hi, body, states)

    states = tuple(jnp.zeros((HEAD_DIM, HEAD_DIM), F32) for _ in range(2 * hg))
    states = scan(0, nc_c, "ctx", states)
    states = scan(0, nc_x // 2, "set", states)
    scan(nc_x // 2, nc_x, "add", states)

    rb = math.gcd(n_x, 256)
    for hh in range(hg):
        lanes = slice(hh * LANES, (hh + 1) * LANES)

        def gate_rows(i, carry):
            rows = pl.ds(pl.multiple_of(i * rb, rb), rb)
            o = o_s[hh, rows, :]
            ms = jnp.mean(o * o, axis=-1, keepdims=True)
            y = o * lax.rsqrt(ms + EPS) * og_ref[...]
            y_ref[rows, lanes] = (y * _silu(zx_ref[rows, lanes])).astype(y_ref.dtype)
            return carry

        lax.fori_loop(0, n_x // rb, gate_rows, 0, unroll=min(4, n_x // rb))


def _gdn_call(px, pc, gcol, grow, conv_w, onorm_g, heads, blk_q, blk_k, blk_v, blk_z):
    b, n_x, _ = px.shape
    n_c = pc.shape[1]
    c = GDN_CHUNK
    n_tot = n_c + n_x
    n_chunks = n_tot // c
    hg = GDN_HEADS_PER_STEP if heads % GDN_HEADS_PER_STEP == 0 else 1
    chunk_group = max(g for g in range(1, GDN_CHUNK_GROUP + 1) if n_chunks % g == 0)
    assert (n_x // c) % 2 == 0 and all(blk % hg == 0 for blk in (blk_q, blk_k, blk_v, blk_z))
    wide = hg * LANES
    masks = jnp.asarray(_tri_masks())

    def seq_spec(n, blk0):
        return pl.BlockSpec((None, n, wide), lambda i, h: (i, 0, blk0 // hg + h))

    def wspec(blk0):
        return pl.BlockSpec((SHORT_CONV, wide), lambda i, h: (0, blk0 // hg + h))

    in_specs = [
        seq_spec(n_x, blk_q), seq_spec(n_x, blk_k), seq_spec(n_x, blk_v), seq_spec(n_x, blk_z),
        seq_spec(n_c, blk_q), seq_spec(n_c, blk_k), seq_spec(n_c, blk_v),
        pl.BlockSpec((None, n_tot, LANES), lambda i, h: (i, 0, 0)),
        pl.BlockSpec((None, hg, n_chunks, SUBLANES, LANES), lambda i, h: (i, h, 0, 0, 0)),
        pl.BlockSpec(masks.shape, lambda i, h: (0, 0, 0)),
        wspec(0), wspec(heads), wspec(2 * heads),
        pl.BlockSpec((1, LANES), lambda i, h: (0, 0)),
    ]
    per_dir = lambda dt: pltpu.VMEM((hg, 2, n_tot, LANES), dt)
    scratch = [pltpu.VMEM((hg, n_tot, LANES), F32)] * 3 + [
        per_dir(F32), per_dir(BF16), per_dir(BF16), per_dir(BF16),
        pltpu.VMEM((hg, n_chunks, c, 2 * c), BF16),
        pltpu.VMEM((hg, 2, n_chunks, SUBLANES, LANES), F32),
        pltpu.VMEM((hg, n_x, LANES), F32),
        pltpu.VMEM((max(n_x, n_c) + 2 * SUBLANES, LANES), F32),
    ]
    return pl.pallas_call(
        functools.partial(_gdn_kernel, heads=heads, hg=hg, chunk_group=chunk_group),
        grid=(b, heads // hg),
        in_specs=in_specs,
        out_specs=pl.BlockSpec((None, n_x, wide), lambda i, h: (i, 0, h)),
        out_shape=jax.ShapeDtypeStruct((b, n_x, heads * HEAD_DIM), BF16),
        scratch_shapes=scratch,
        compiler_params=_params("arbitrary", "arbitrary"),
        name="gdn",
    )(px, px, px, px, pc, pc, pc, gcol, grow, masks, conv_w, conv_w, conv_w, onorm_g.reshape(1, LANES))


def _pool_kernel(p_ref, w_ref, s_ref, o_ref, pad_s):
    n = p_ref.shape[0]
    gc = w_ref.shape[1]
    slabs = pad_s.shape[0]
    halo = pad_s.shape[1] - n
    half = halo // 2
    for sl in range(slabs):
        pad_s[sl, 0:half, :] = jnp.zeros((half, LANES), F32)
        pad_s[sl, half + n:halo + n, :] = jnp.zeros((half, LANES), F32)
    rb = math.gcd(n, 256)
    for gi, r in enumerate(POOL_RADII):
        cols = slice(gi * gc, (gi + 1) * gc)

        def copy_rows(i, carry):
            r0 = pl.multiple_of(i * rb, rb)
            for sl in range(slabs):
                lanes = slice(gi * gc + sl * LANES, gi * gc + (sl + 1) * LANES)
                pad_s[sl, pl.ds(pl.multiple_of(half + r0, SUBLANES), rb), :] = p_ref[pl.ds(r0, rb), lanes]
            return carry

        def pool_rows(i, carry):
            r0 = pl.multiple_of(i * rb, rb)
            tr = lax.broadcasted_iota(jnp.int32, (rb, 1), 0) + r0
            cnt = (jnp.minimum(tr + r + 1, n) - jnp.maximum(tr - r, 0)).astype(F32)
            parts = []
            for sl in range(slabs):
                tot = pad_s[sl, pl.ds(half - r + r0, rb), :]
                for s in range(1 - r, r + 1):
                    tot = tot + pad_s[sl, pl.ds(half + s + r0, rb), :]
                centre = pad_s[sl, pl.ds(pl.multiple_of(half + r0, SUBLANES), rb), :]
                parts.append((tot / cnt - centre).astype(BF16))
            y = jnp.dot(jnp.concatenate(parts, axis=1), w_ref[gi], preferred_element_type=F32)
            o_ref[pl.ds(r0, rb), cols] = (y * s_ref[:, cols]).astype(o_ref.dtype)
            return carry

        lax.fori_loop(0, n // rb, copy_rows, 0)
        lax.fori_loop(0, n // rb, pool_rows, 0, unroll=min(2, n // rb))


def _pool_call(px, col_block, pool_w, pool_scale):
    b, n, _ = px.shape
    g, gc, _ = pool_w.shape
    width = g * gc
    return pl.pallas_call(
        _pool_kernel,
        grid=(b,),
        in_specs=[
            pl.BlockSpec((None, n, width), lambda i: (i, 0, col_block)),
            pl.BlockSpec((g, gc, gc), lambda i: (0, 0, 0)),
            pl.BlockSpec((1, width), lambda i: (0, 0)),
        ],
        out_specs=pl.BlockSpec((None, n, width), lambda i: (i, 0, 0)),
        out_shape=jax.ShapeDtypeStruct((b, n, width), BF16),
        scratch_shapes=[pltpu.VMEM((gc // LANES, n + 2 * SUBLANES * pl.cdiv(max(POOL_RADII), SUBLANES), LANES), F32)],
        compiler_params=_params("arbitrary"),
        name="pool",
    )(px, pool_w, pool_scale.reshape(1, width))


def _sgu_kernel(u_ref, v_ref, lg_ref, lb_ref, ws_ref, bs_ref, o_ref, vn_ref):
    tr, wdt = v_ref.shape
    gw = wdt // SGU_GROUPS
    rc = 64

    def ln_body(r, carry):
        rows = pl.ds(pl.multiple_of(r * rc, rc), rc)
        v = v_ref[rows, :]
        mu = jnp.mean(v, axis=-1, keepdims=True)
        d = v - mu
        var = jnp.mean(d * d, axis=-1, keepdims=True)
        vn_ref[rows, :] = (d * lax.rsqrt(var + EPS) * lg_ref[...] + lb_ref[...]).astype(BF16)
        return carry

    lax.fori_loop(0, tr // rc, ln_body, 0)
    for ch in range(tr // SGU_CHUNK):
        rows = slice(ch * SGU_CHUNK, (ch + 1) * SGU_CHUNK)
        for g in range(SGU_GROUPS):
            cols = slice(g * gw, (g + 1) * gw)
            mixed = jnp.dot(ws_ref[g], vn_ref[rows, cols], preferred_element_type=F32) + bs_ref[:, g:g + 1]
            o_ref[rows, cols] = (u_ref[rows, cols] * mixed).astype(o_ref.dtype)


def _sgu_call(zz, ln_g, ln_b, ws, bs_t, tr):
    b, n, w2 = zz.shape
    wdt = w2 // 2
    return pl.pallas_call(
        _sgu_kernel,
        grid=(b, n // tr),
        in_specs=[
            pl.BlockSpec((None, tr, wdt), lambda i, r: (i, r, 0)),
            pl.BlockSpec((None, tr, wdt), lambda i, r: (i, r, 1)),
            pl.BlockSpec((1, wdt), lambda i, r: (0, 0)),
            pl.BlockSpec((1, wdt), lambda i, r: (0, 0)),
            pl.BlockSpec((SGU_GROUPS, SGU_CHUNK, SGU_CHUNK), lambda i, r: (0, 0, 0)),
            pl.BlockSpec((SGU_CHUNK, SGU_GROUPS), lambda i, r: (0, 0)),
        ],
        out_specs=pl.BlockSpec((None, tr, wdt), lambda i, r: (i, r, 0)),
        out_shape=jax.ShapeDtypeStruct((b, n, wdt), BF16),
        scratch_shapes=[pltpu.VMEM((tr, wdt), BF16)],
        compiler_params=_params("arbitrary", "arbitrary"),
        name="sgu",
    )(zz, zz, ln_g.reshape(1, wdt), ln_b.reshape(1, wdt), ws, bs_t)


def _grid_pos_embed(n, d):
    rows = n // GRID_W
    nf = d // 4
    omega = 1.0 / (10000.0 ** (jnp.arange(nf, dtype=F32) / nf))
    r = jnp.repeat(jnp.arange(rows, dtype=F32), GRID_W)
    col = jnp.tile(jnp.arange(GRID_W, dtype=F32), rows)
    ar = r[:, None] * omega
    ac = col[:, None] * omega
    return jnp.concatenate([jnp.sin(ar), jnp.cos(ar), jnp.sin(ac), jnp.cos(ac)], axis=-1)


def _pick_tile(n, candidates):
    for t in candidates:
        if n % t == 0:
            return t
    raise ValueError(f"no tile for {n}")


def _gate_rows(gcol, heads):
    b, n, _ = gcol.shape
    c = GDN_CHUNK
    g = gcol[:, :, :4 * heads].reshape(b, n // c, c, 2, 2, heads)
    g = jnp.transpose(g, (0, 5, 1, 3, 4, 2)).reshape(b, heads, n // c, 2, 2 * c)
    return jnp.pad(g, ((0, 0), (0, 0), (0, 0), (0, SUBLANES - 2), (0, 0)))


def _mlp(h, mods, layer, norm_g, w1, w2, final_g, n_seq, batch_row):
    tm = _pick_tile(n_seq, (1024, 512, 256, 128))
    hid = _nm_call(h, None, norm_g, mods, layer, 3, batch_row(tm), w1, act="relu2", out_dtype=BF16,
                   tm=tm, tn=_pick_tile(w1.shape[1], (2048, 1024, 512, 256, 128)), rows_per_seq=n_seq,
                   name=f"mlp_up{layer}")
    tm2 = _pick_tile(n_seq, (512, 256, 128))
    return _mr_call(hid, w2, h, None, mods, layer, 5, final_g, tm=tm2,
                    tk=_pick_tile(w2.shape[0], (2048, 1024, 512, 256, 128)), rows_per_seq=n_seq,
                    name=f"mlp_down{layer}")


def kernel(x, c, ctx, c_ctx, ada_w, ada_b, norm1_g, norm2_g, mlp_w1, mlp_w2, ev_w_in, ev_conv_w, ev_a_log,
           ev_dt_bias, ev_onorm_g, ev_pool_w, ev_pool_scale, ev_w_out, od_w_in, od_ln_g, od_ln_b, od_ws, od_bs,
           od_w_out, final_g):
    b, n_seq, d = x.shape
    n_ctx = ctx.shape[1]
    depth = ada_w.shape[0]
    heads = ev_a_log.shape[-1]
    qk_w = heads * HEAD_DIM
    gdn_qkv = 3 * qk_w
    gdn_in = gdn_qkv + 4 * heads
    pool_w_dim = ev_pool_scale.shape[-1]
    assert b + 1 <= MOD_ROWS and 4 * heads <= LANES and n_seq % GDN_CHUNK == 0 and n_ctx % GDN_CHUNK == 0
    assert qk_w % 1024 == 0 and pool_w_dim == qk_w

    cc = jnp.zeros((MOD_ROWS, d), F32).at[:b].set(c).at[b].set(c_ctx)
    mods = _ada_call(cc, ada_w, ada_b).reshape(depth, MOD_ROWS, N_ADA, 1, d)

    def batch_row(tm):
        return lambda i: (i * tm) // n_seq

    pos = _grid_pos_embed(n_seq, d)
    h = x.reshape(b * n_seq, d)
    last_even = 2 * ((depth - 1) // 2)
    assert last_even == 0, "context stream is only advanced through its first DeltaNet layer"

    for i in range(depth):
        j = i // 2
        fin = final_g if i == depth - 1 else None
        if i % 2 == 0:
            w_in = ev_w_in[j]
            ab_pad = LANES - 4 * heads
            n_cols = 5 * qk_w + LANES
            n_pad = -n_cols % 768
            w_x = jnp.concatenate([
                w_in[:, :gdn_qkv], w_in[:, gdn_in:gdn_in + 2 * qk_w], w_in[:, gdn_qkv:gdn_in],
                jnp.zeros((d, ab_pad + n_pad), F32)], axis=1).astype(BF16)
            w_c = jnp.concatenate([w_in[:, :gdn_in], jnp.zeros((d, ab_pad), F32)], axis=1).astype(BF16)
            blk = qk_w // LANES
            tm = _pick_tile(n_seq, (1024, 512, 256, 128))
            px = _nm_call(h, pos, norm1_g[i], mods, i, 0, batch_row(tm), w_x, act="none", out_dtype=F32,
                          tm=tm, tn=768, rows_per_seq=n_seq, name="even_in")
            tmc = _pick_tile(b * n_ctx, (1024, 512, 256, 128))
            pc = _nm_call(ctx.reshape(b * n_ctx, d), None, norm1_g[i], mods, i, 0, lambda t: b, w_c, act="none",
                          out_dtype=F32, tm=tmc, tn=_pick_tile(w_c.shape[1], (640, 128)), rows_per_seq=b * n_ctx,
                          name="even_in_ctx")
            px = px.reshape(b, n_seq, -1)
            pc = pc.reshape(b, n_ctx, -1)

            lane_pad = jnp.zeros((LANES - 2 * heads,), F32)
            alog_row = jnp.concatenate([ev_a_log[j].reshape(-1), lane_pad]).reshape(1, LANES)
            dtb_row = jnp.concatenate([ev_dt_bias[j].reshape(-1), lane_pad]).reshape(1, LANES)
            gcol = jnp.concatenate([_gates_call(pc, 3 * blk, alog_row, dtb_row, heads),
                                    _gates_call(px, 5 * blk, alog_row, dtb_row, heads)], axis=1)
            y_a = _gdn_call(px, pc, gcol, _gate_rows(gcol, heads), ev_conv_w[j], ev_onorm_g[j],
                            heads, 0, blk, 2 * blk, 3 * blk)
            y_b = _pool_call(px, 4 * qk_w // pool_w_dim, ev_pool_w[j].astype(BF16), ev_pool_scale[j])
            y = jnp.concatenate([y_a, y_b], axis=-1).reshape(b * n_seq, -1)
            w_out = ev_w_out[j].astype(BF16)
        else:
            tm = _pick_tile(n_seq, (1024, 512, 256, 128))
            w_od = od_w_in[j].astype(BF16)
            zz = _nm_call(h, None, norm1_g[i], mods, i, 0, batch_row(tm), w_od, act="gelu", out_dtype=F32,
                          tm=tm, tn=_pick_tile(w_od.shape[1], (1024, 512, 256, 128)), rows_per_seq=n_seq,
                          name="odd_in")
            y = _sgu_call(zz.reshape(b, n_seq, -1), od_ln_g[j], od_ln_b[j], od_ws[j].astype(BF16),
                          jnp.transpose(od_bs[j]), _pick_tile(n_seq, (512, 256, 128)))
            y = y.reshape(b * n_seq, -1)
            w_out = od_w_out[j].astype(BF16)
        tm2 = _pick_tile(n_seq, (512, 256, 128))
        h = _mr_call(y, w_out, h, pos, mods, i, 2, None, tm=tm2,
                     tk=_pick_tile(y.shape[1], (2048, 1024, 512, 256, 128)), rows_per_seq=n_seq, name=f"mix_out{i}")
        pos = None
        h = _mlp(h, mods, i, norm2_g[i], mlp_w1[i].astype(BF16), mlp_w2[i].astype(BF16), fin, n_seq, batch_row)
    return h.reshape(b, n_seq, d)
```

```python
import functools
import math

import jax
import jax.numpy as jnp
import numpy as np
from jax import lax
from jax.experimental import pallas as pl
from jax.experimental.pallas import tpu as pltpu

F32 = jnp.float32
BF16 = jnp.bfloat16

EPS = 1e-6
GRID_W = 64
N_ADA = 6
HEAD_DIM = 128
GDN_CHUNK = 64
SHORT_CONV = 5
POOL_RADII = (1, 2, 4, 8)
SGU_GROUPS = 4
SGU_CHUNK = 128

LANES = 128
SUBLANES = 8
MOD_ROWS = 32
VMEM_LIMIT = 56 * 1024 * 1024

GDN_HEADS_PER_STEP = 2
GDN_CHUNK_GROUP = 9


def _params(*sem):
    return pltpu.CompilerParams(dimension_semantics=sem, vmem_limit_bytes=VMEM_LIMIT)


def _silu(x):
    return x * jax.nn.sigmoid(x)


def _ada_kernel(c_ref, w_ref, b_ref, o_ref):
    s = _silu(c_ref[...]).astype(BF16)
    o_ref[...] = jnp.dot(s, w_ref[...].astype(BF16), preferred_element_type=F32) + b_ref[...]


def _ada_call(cc, ada_w, ada_b):
    depth, d, n = ada_w.shape
    tn = 1024
    return pl.pallas_call(
        _ada_kernel,
        grid=(depth, n // tn),
        in_specs=[
            pl.BlockSpec((MOD_ROWS, d), lambda l, j: (0, 0)),
            pl.BlockSpec((None, d, tn), lambda l, j: (l, 0, j)),
            pl.BlockSpec((None, 1, tn), lambda l, j: (l, 0, j)),
        ],
        out_specs=pl.BlockSpec((None, MOD_ROWS, tn), lambda l, j: (l, 0, j)),
        out_shape=jax.ShapeDtypeStruct((depth, MOD_ROWS, n), F32),
        compiler_params=_params("arbitrary", "arbitrary"),
        name="ada",
    )(cc, ada_w, ada_b.reshape(depth, 1, n))


def _act(y, act):
    if act == "relu2":
        r = jnp.maximum(y, 0.0)
        return r * r
    if act == "gelu":
        return 0.5 * y * (1.0 + lax.erf(y * math.sqrt(0.5)))
    return y


def _nm_kernel(*refs, has_pos, act, row_chunk, keep):
    x_ref = refs[0]
    rest = list(refs[1:])
    pos_ref = rest.pop(0) if has_pos else None
    g_ref, sh_ref, sc_ref, w_ref, o_ref = rest[:5]
    keep_ref = rest[5] if keep is not None else None
    xn_ref = rest[-1]
    tm = x_ref.shape[0]

    @pl.when(pl.program_id(1) == 0)
    def _():
        gm = g_ref[...] * (1.0 + sc_ref[...])
        shift = sh_ref[...]

        def body(r, carry):
            rows = pl.ds(pl.multiple_of(r * row_chunk, row_chunk), row_chunk)
            x = x_ref[rows, :]
            if has_pos:
                x = x + pos_ref[rows, :]
            ms = jnp.mean(x * x, axis=-1, keepdims=True)
            xn_ref[rows, :] = (x * lax.rsqrt(ms + EPS) * gm + shift).astype(BF16)
            return carry

        lax.fori_loop(0, tm // row_chunk, body, 0, unroll=8)

    acc = jnp.dot(xn_ref[...], w_ref[...], preferred_element_type=F32)
    o_ref[...] = _act(acc, act).astype(o_ref.dtype)
    if keep is not None:
        keep_tile, keep_off = keep

        @pl.when(pl.program_id(1) == keep_tile)
        def _():
            keep_ref[...] = acc[:, keep_off:keep_off + LANES]


def _seq_major(i, tiles_per_seq, n_seqs):
    seq = i % n_seqs
    tile = i // n_seqs
    return seq * tiles_per_seq + tile, seq, tile


def _nm_call(x, pos, gain, mods, layer, k_shift, mod_row, w, *, act, out_dtype, tm, tn, rows_per_seq, name,
             keep_col=None):
    m, d = x.shape
    n = w.shape[1]
    has_pos = pos is not None
    tiles_per_seq = rows_per_seq // tm
    n_seqs = m // rows_per_seq

    def row_tile(i):
        return _seq_major(i, tiles_per_seq, n_seqs)[0] if has_pos else i

    in_specs = [pl.BlockSpec((tm, d), lambda i, j: (row_tile(i), 0))]
    args = [x]
    if has_pos:
        in_specs.append(pl.BlockSpec((tm, d), lambda i, j: (_seq_major(i, tiles_per_seq, n_seqs)[2], 0)))
        args.append(pos)
    in_specs += [
        pl.BlockSpec((1, d), lambda i, j: (0, 0)),
        pl.BlockSpec((None, None, None, 1, d), lambda i, j: (layer, mod_row(row_tile(i)), k_shift, 0, 0)),
        pl.BlockSpec((None, None, None, 1, d), lambda i, j: (layer, mod_row(row_tile(i)), k_shift + 1, 0, 0)),
        pl.BlockSpec((d, tn), lambda i, j: (0, j)),
    ]
    args += [gain.reshape(1, d), mods, mods, w]
    out_specs = [pl.BlockSpec((tm, tn), lambda i, j: (row_tile(i), j))]
    out_shape = [jax.ShapeDtypeStruct((m, n), out_dtype)]
    keep = None
    if keep_col is not None:
        keep = (keep_col // tn, keep_col % tn)
        assert keep[1] + LANES <= tn
        out_specs.append(pl.BlockSpec((tm, LANES), lambda i, j: (row_tile(i), 0)))
        out_shape.append(jax.ShapeDtypeStruct((m, LANES), F32))
    res = pl.pallas_call(
        functools.partial(_nm_kernel, has_pos=has_pos, act=act, row_chunk=16, keep=keep),
        grid=(m // tm, n // tn),
        in_specs=in_specs,
        out_specs=out_specs,
        out_shape=out_shape,
        scratch_shapes=[pltpu.VMEM((tm, d), BF16)],
        compiler_params=_params("arbitrary", "arbitrary"),
        name=name,
    )(*args)
    return res if keep_col is not None else res[0]


def _mr_kernel(*refs, has_pos, has_final, k_steps):
    a_ref, w_ref, h_ref, gate_ref = refs[:4]
    rest = list(refs[4:])
    pos_ref = rest.pop(0) if has_pos else None
    fg_ref = rest.pop(0) if has_final else None
    o_ref, *acc = rest

    def finish(total):
        h = h_ref[...]
        if has_pos:
            h = h + pos_ref[...]
        y = h + gate_ref[...] * total
        if has_final:
            ms = jnp.mean(y * y, axis=-1, keepdims=True)
            y = y * lax.rsqrt(ms + EPS) * fg_ref[...]
        o_ref[...] = y

    if k_steps == 1:
        finish(jnp.dot(a_ref[...], w_ref[...], preferred_element_type=F32))
        return
    acc_ref, = acc
    k = pl.program_id(1)

    @pl.when(k == 0)
    def _():
        acc_ref[...] = jnp.zeros_like(acc_ref)

    acc_ref[...] += jnp.dot(a_ref[...], w_ref[...], preferred_element_type=F32)

    @pl.when(k == k_steps - 1)
    def _():
        finish(acc_ref[...])


def _mr_call(a, w, h, pos, mods, layer, k_gate, final_g, *, tm, tk, rows_per_seq, name):
    m, kdim = a.shape
    d = w.shape[1]
    has_final = final_g is not None
    has_pos = pos is not None
    tiles_per_seq = rows_per_seq // tm
    n_seqs = m // rows_per_seq

    def row_tile(i):
        return _seq_major(i, tiles_per_seq, n_seqs)[0] if has_pos else i

    in_specs = [
        pl.BlockSpec((tm, tk), lambda i, k: (row_tile(i), k)),
        pl.BlockSpec((tk, d), lambda i, k: (k, 0)),
        pl.BlockSpec((tm, d), lambda i, k: (row_tile(i), 0)),
        pl.BlockSpec((None, None, None, 1, d), lambda i, k: (layer, row_tile(i) // tiles_per_seq, k_gate, 0, 0)),
    ]
    args = [a, w, h, mods]
    if has_pos:
        in_specs.append(pl.BlockSpec((tm, d), lambda i, k: (_seq_major(i, tiles_per_seq, n_seqs)[2], 0)))
        args.append(pos)
    if has_final:
        in_specs.append(pl.BlockSpec((1, d), lambda i, k: (0, 0)))
        args.append(final_g.reshape(1, d))
    k_steps = kdim // tk
    return pl.pallas_call(
        functools.partial(_mr_kernel, has_pos=has_pos, has_final=has_final, k_steps=k_steps),
        grid=(m // tm, k_steps),
        in_specs=in_specs,
        out_specs=pl.BlockSpec((tm, d), lambda i, k: (row_tile(i), 0)),
        out_shape=jax.ShapeDtypeStruct((m, d), F32),
        scratch_shapes=[pltpu.VMEM((tm, d), F32)] if k_steps > 1 else [],
        compiler_params=_params("arbitrary", "arbitrary"),
        name=name,
    )(*args)


def _gates_kernel(ab_ref, alog_ref, dtb_ref, o_ref, *, heads):
    x = ab_ref[...]
    n = x.shape[0]
    lane = lax.broadcasted_iota(jnp.int32, x.shape, 1)
    row = lax.broadcasted_iota(jnp.int32, x.shape, 0)
    pos = row % GDN_CHUNK
    g = -jnp.exp(alog_ref[...]) * jax.nn.softplus(x + dtb_ref[...])
    pre = g
    suf = g
    s = 1
    while s < GDN_CHUNK:
        pre = pre + jnp.where(pos >= s, pltpu.roll(pre, s, 0), 0.0)
        suf = suf + jnp.where(pos < GDN_CHUNK - s, pltpu.roll(suf, n - s, 0), 0.0)
        s *= 2
    beta = jax.nn.sigmoid(x)
    o_ref[...] = jnp.where(lane < heads, pre, jnp.where(lane < 2 * heads, suf, beta))


def _gates_call(p, col_block, alog_row, dtb_row, heads):
    b, n, _ = p.shape
    return pl.pallas_call(
        functools.partial(_gates_kernel, heads=heads),
        grid=(b,),
        in_specs=[
            pl.BlockSpec((None, n, LANES), lambda i: (i, 0, col_block)),
            pl.BlockSpec((1, LANES), lambda i: (0, 0)),
            pl.BlockSpec((1, LANES), lambda i: (0, 0)),
        ],
        out_specs=pl.BlockSpec((None, n, LANES), lambda i: (i, 0, 0)),
        out_shape=jax.ShapeDtypeStruct((b, n, LANES), F32),
        compiler_params=_params("arbitrary"),
        name="gdn_gates",
    )(p, alog_row, dtb_row)


def _l2norm(t):
    return t * lax.rsqrt(jnp.sum(t * t, axis=-1, keepdims=True) + EPS)


def _dot_nt(a, b):
    return lax.dot_general(a, b, (((1,), (1,)), ((), ())), preferred_element_type=F32)


def _dot_tn(a, b):
    return lax.dot_general(a, b, (((0,), (0,)), ((), ())), preferred_element_type=F32)


def _dot(a, b):
    return jnp.dot(a, b, preferred_element_type=F32)


def _tri_masks():
    c = GDN_CHUNK
    r = np.arange(c)[:, None]
    col = np.arange(2 * c)[None, :]
    j = col % c
    bwd = col >= c
    hi = np.where(bwd, j, r)
    lo = np.where(bwd, r, j)
    masks = [hi >= lo, hi > lo, r == j]
    s = 1
    while s < c:
        masks.append(((r ^ j) < 2 * s) & ((hi & s) != 0) & ((lo & s) == 0))
        s *= 2
    return np.stack(masks).astype(np.float32)


def _block_diag(x):
    c = x.shape[0]
    lane = lax.broadcasted_iota(jnp.int32, x.shape, 1)
    top = jnp.where(lane < c, x, 0.0).astype(BF16)
    bot = jnp.where(lane >= c, x, 0.0).astype(BF16)
    return jnp.concatenate([top, bot], axis=0)


def _twice_diag(x16):
    z = jnp.zeros_like(x16)
    return jnp.concatenate([jnp.concatenate([x16, z], axis=1), jnp.concatenate([z, x16], axis=1)], axis=0)


def _gdn_kernel(qx_ref, kx_ref, vx_ref, zx_ref, qc_ref, kc_ref, vc_ref, gcol_ref, grow_ref, mask_ref,
                wq_ref, wk_ref, wv_ref, og_ref, y_ref,
                q_s, k_s, v_s, u_s, w_s, qd_s, kd_s, qkm_s, dl_s, o_s, pad_s, *, heads, hg, chunk_group):
    hblk = pl.program_id(1)
    n_x = qx_ref.shape[0]
    n_c = qc_ref.shape[0]
    c = GDN_CHUNK
    nc_c, nc_x = n_c // c, n_x // c
    n_chunks = nc_c + nc_x
    n_levels = mask_ref.shape[0] - 3

    halo = SUBLANES
    pad_s[0:halo, :] = jnp.zeros((halo, LANES), F32)
    for hh in range(hg):
        lanes = slice(hh * LANES, (hh + 1) * LANES)
        for src_c, src_x, dst, w_ref, kind in ((qc_ref, qx_ref, q_s, wq_ref, "q"), (kc_ref, kx_ref, k_s, wk_ref, "k"),
                                               (vc_ref, vx_ref, v_s, wv_ref, "v")):
            w = w_ref[:, lanes]
            for src, base, n in ((src_c, 0, n_c), (src_x, n_c, n_x)):
                rb = math.gcd(n, 256)

                def copy_rows(i, carry, src=src):
                    r0 = pl.multiple_of(i * rb, rb)
                    pad_s[pl.ds(pl.multiple_of(halo + r0, halo), rb), :] = src[pl.ds(r0, rb), lanes].astype(F32)
                    return carry

                def conv_rows(i, carry, base=base, w=w, kind=kind, dst=dst):
                    r0 = pl.multiple_of(i * rb, rb)
                    first = halo - (SHORT_CONV - 1) // 2 + r0
                    t = w[0:1, :] * pad_s[pl.ds(first, rb), :]
                    for tap in range(1, SHORT_CONV):
                        t = t + w[tap:tap + 1, :] * pad_s[pl.ds(first + tap, rb), :]
                    t = _silu(t)
                    if kind == "q":
                        t = _l2norm(t) * (HEAD_DIM ** -0.5)
                    elif kind == "k":
                        t = _l2norm(t)
                    dst[hh, pl.ds(pl.multiple_of(base + r0, halo), rb), :] = t
                    return carry

                lax.fori_loop(0, n // rb, copy_rows, 0)
                pad_s[halo + n:2 * halo + n, :] = jnp.zeros((halo, LANES), F32)
                lax.fori_loop(0, n // rb, conv_rows, 0, unroll=min(4, n // rb))

    glane = lax.broadcasted_iota(jnp.int32, (c, LANES), 1)
    left = glane < c

    def wy_body(it, carry):
        items = []
        for hh in range(hg):
            for j in range(chunk_group):
                ch = it * chunk_group + j
                rows = pl.ds(pl.multiple_of(ch * c, c), c)
                head = hblk * hg + hh
                blk = gcol_ref[rows, :]

                def col(idx, blk=blk):
                    return jnp.broadcast_to(jnp.sum(jnp.where(glane == idx, blk, 0.0), axis=-1, keepdims=True), (c, LANES))

                k = k_s[hh, rows, :]
                k16 = k.astype(BF16)
                items.append(dict(hh=hh, ch=ch, rows=rows, k=k, k16=k16, q=q_s[hh, rows, :], v=v_s[hh, rows, :],
                                  gf=col(head), gb=col(heads + head), bf=col(2 * heads + head),
                                  bb=col(3 * heads + head), rp=grow_ref[hh, ch]))
        raws = [_dot_nt(jnp.concatenate([d["k16"], d["q"].astype(BF16)], axis=0),
                        jnp.concatenate([d["k16"], d["k16"]], axis=0)) for d in items]
        for d, raw in zip(items, raws):
            gam_col = jnp.where(left, d["gf"], d["gb"])
            beta_col = jnp.where(left, d["bf"], d["bb"])
            incl = mask_ref[0]
            diff = gam_col - d["rp"][0:1, :]
            decay = jnp.exp(jnp.where(incl > 0.0, diff, 0.0)) * incl
            d["a"] = raw[:c] * decay * beta_col * mask_ref[1]
            qkm_s[d["hh"], d["ch"]] = (raw[c:] * decay).astype(BF16)
            d["t"] = mask_ref[2] - d["a"] * mask_ref[3]
        for lvl in range(1, n_levels):
            xs = [_dot((d["a"] * mask_ref[3 + lvl]).astype(BF16), _block_diag(d["t"])) for d in items]
            ys = [_dot(d["t"].astype(BF16), _block_diag(x)) for d, x in zip(items, xs)]
            for d, y in zip(items, ys):
                d["t"] = d["t"] - y
        us, ws = [], []
        for d in items:
            beta_row = d["rp"][1:2, :]
            scale_w = beta_row * jnp.exp(d["rp"][0:1, :])
            us.append(_dot((d["t"] * beta_row).astype(BF16), _twice_diag(d["v"].astype(BF16))))
            ws.append(_dot((d["t"] * scale_w).astype(BF16), _twice_diag(d["k16"])))
        for d, u, w in zip(items, us, ws):
            hh, ch, rows = d["hh"], d["ch"], d["rows"]
            for dirn, gam in ((0, d["gf"]), (1, d["gb"])):
                lanes = slice(dirn * LANES, (dirn + 1) * LANES)
                g_last = gam[c - 1:c, :] if dirn == 0 else gam[0:1, :]
                u_s[hh, dirn, rows, :] = u[:, lanes]
                w_s[hh, dirn, rows, :] = w[:, lanes].astype(BF16)
                qd_s[hh, dirn, rows, :] = (d["q"] * jnp.exp(gam)).astype(BF16)
                kd_s[hh, dirn, rows, :] = (d["k"] * jnp.exp(g_last - gam)).astype(BF16)
                dl_s[hh, dirn, ch] = jnp.broadcast_to(jnp.exp(g_last), (SUBLANES, LANES))
        return carry

    lax.fori_loop(0, n_chunks // chunk_group, wy_body, 0)

    def scan(lo, hi, mode, states):
        def body(i, states):
            chains = []
            for hh in range(hg):
                for dirn in range(2):
                    if mode == "ctx":
                        ch = i if dirn == 0 else nc_c - 1 - i
                    else:
                        ch = nc_c + (i if dirn == 0 else nc_x - 1 - i)
                    rows = pl.ds(pl.multiple_of(ch * c, c), c)
                    chains.append((hh, dirn, ch, rows))
            s16 = [s.astype(BF16) for s in states]
            if mode == "ctx":
                r1 = [_dot(w_s[hh, dirn, rows, :], s) for (hh, dirn, ch, rows), s in zip(chains, s16)]
            else:
                r1 = [_dot(jnp.concatenate([w_s[hh, dirn, rows, :], qd_s[hh, dirn, rows, :]], axis=0), s)
                      for (hh, dirn, ch, rows), s in zip(chains, s16)]
            vn16 = [(u_s[hh, dirn, rows, :] - r[:c]).astype(BF16) for (hh, dirn, ch, rows), r in zip(chains, r1)]
            upd = [_dot_tn(kd_s[hh, dirn, rows, :], vn) for (hh, dirn, ch, rows), vn in zip(chains, vn16)]
            if mode != "ctx":
                zero = jnp.zeros((c, LANES), BF16)
                intra = [_dot(qkm_s[hh, ch], jnp.concatenate([vn, zero] if dirn == 0 else [zero, vn], axis=0))
                         for (hh, dirn, ch, rows), vn in zip(chains, vn16)]
                for (hh, dirn, ch, rows), r, o in zip(chains, r1, intra):
                    orow = pl.ds(pl.multiple_of((ch - nc_c) * c, c), c)
                    if mode == "set":
                        o_s[hh, orow, :] = r[c:] + o
                    else:
                        o_s[hh, orow, :] += r[c:] + o
            return tuple(s * dl_s[hh, dirn, ch][0:1, :] + up
                         for (hh, dirn, ch, rows), s, up in zip(chains, states, upd))

        return lax.fori_loop(lo, hi, body, states)

    states = tuple(jnp.zeros((HEAD_DIM, HEAD_DIM), F32) for _ in range(2 * hg))
    states = scan(0, nc_c, "ctx", states)
    states = scan(0, nc_x // 2, "set", states)
    scan(nc_x // 2, nc_x, "add", states)

    rb = math.gcd(n_x, 256)
    for hh in range(hg):
        lanes = slice(hh * LANES, (hh + 1) * LANES)

        def gate_rows(i, carry):
            rows = pl.ds(pl.multiple_of(i * rb, rb), rb)
            o = o_s[hh, rows, :]
            ms = jnp.mean(o * o, axis=-1, keepdims=True)
            y = o * lax.rsqrt(ms + EPS) * og_ref[...]
            y_ref[rows, lanes] = (y * _silu(zx_ref[rows, lanes].astype(F32))).astype(y_ref.dtype)
            return carry

        lax.fori_loop(0, n_x // rb, gate_rows, 0, unroll=min(4, n_x // rb))


def _gdn_call(px, pc, gcol, grow, conv_w, onorm_g, heads, blk_q, blk_k, blk_v, blk_z):
    b, n_x, _ = px.shape
    n_c = pc.shape[1]
    c = GDN_CHUNK
    n_tot = n_c + n_x
    n_chunks = n_tot // c
    hg = GDN_HEADS_PER_STEP if heads % GDN_HEADS_PER_STEP == 0 else 1
    chunk_group = max(g for g in range(1, GDN_CHUNK_GROUP + 1) if n_chunks % g == 0)
    assert (n_x // c) % 2 == 0 and all(blk % hg == 0 for blk in (blk_q, blk_k, blk_v, blk_z))
    wide = hg * LANES
    masks = jnp.asarray(_tri_masks())

    def seq_spec(n, blk0):
        return pl.BlockSpec((None, n, wide), lambda i, h: (i, 0, blk0 // hg + h))

    def wspec(blk0):
        return pl.BlockSpec((SHORT_CONV, wide), lambda i, h: (0, blk0 // hg + h))

    in_specs = [
        seq_spec(n_x, blk_q), seq_spec(n_x, blk_k), seq_spec(n_x, blk_v), seq_spec(n_x, blk_z),
        seq_spec(n_c, blk_q), seq_spec(n_c, blk_k), seq_spec(n_c, blk_v),
        pl.BlockSpec((None, n_tot, LANES), lambda i, h: (i, 0, 0)),
        pl.BlockSpec((None, hg, n_chunks, SUBLANES, LANES), lambda i, h: (i, h, 0, 0, 0)),
        pl.BlockSpec(masks.shape, lambda i, h: (0, 0, 0)),
        wspec(0), wspec(heads), wspec(2 * heads),
        pl.BlockSpec((1, LANES), lambda i, h: (0, 0)),
    ]
    per_dir = lambda dt: pltpu.VMEM((hg, 2, n_tot, LANES), dt)
    scratch = [pltpu.VMEM((hg, n_tot, LANES), F32)] * 3 + [
        per_dir(F32), per_dir(BF16), per_dir(BF16), per_dir(BF16),
        pltpu.VMEM((hg, n_chunks, c, 2 * c), BF16),
        pltpu.VMEM((hg, 2, n_chunks, SUBLANES, LANES), F32),
        pltpu.VMEM((hg, n_x, LANES), F32),
        pltpu.VMEM((max(n_x, n_c) + 2 * SUBLANES, LANES), F32),
    ]
    return pl.pallas_call(
        functools.partial(_gdn_kernel, heads=heads, hg=hg, chunk_group=chunk_group),
        grid=(b, heads // hg),
        in_specs=in_specs,
        out_specs=pl.BlockSpec((None, n_x, wide), lambda i, h: (i, 0, h)),
        out_shape=jax.ShapeDtypeStruct((b, n_x, heads * HEAD_DIM), BF16),
        scratch_shapes=scratch,
        compiler_params=_params("arbitrary", "arbitrary"),
        name="gdn",
    )(px, px, px, px, pc, pc, pc, gcol, grow, masks, conv_w, conv_w, conv_w, onorm_g.reshape(1, LANES))


def _pool_kernel(p_ref, w_ref, s_ref, o_ref, pad_s):
    n = p_ref.shape[0]
    gc = w_ref.shape[1]
    slabs = pad_s.shape[0]
    halo = pad_s.shape[1] - n
    half = halo // 2
    for sl in range(slabs):
        pad_s[sl, 0:half, :] = jnp.zeros((half, LANES), F32)
        pad_s[sl, half + n:halo + n, :] = jnp.zeros((half, LANES), F32)
    rb = math.gcd(n, 256)
    for gi, r in enumerate(POOL_RADII):
        cols = slice(gi * gc, (gi + 1) * gc)

        def copy_rows(i, carry):
            r0 = pl.multiple_of(i * rb, rb)
            for sl in range(slabs):
                lanes = slice(gi * gc + sl * LANES, gi * gc + (sl + 1) * LANES)
                pad_s[sl, pl.ds(pl.multiple_of(half + r0, SUBLANES), rb), :] = p_ref[pl.ds(r0, rb), lanes].astype(F32)
            return carry

        def pool_rows(i, carry):
            r0 = pl.multiple_of(i * rb, rb)
            tr = lax.broadcasted_iota(jnp.int32, (rb, 1), 0) + r0
            cnt = (jnp.minimum(tr + r + 1, n) - jnp.maximum(tr - r, 0)).astype(F32)
            parts = []
            for sl in range(slabs):
                tot = pad_s[sl, pl.ds(half - r + r0, rb), :]
                for s in range(1 - r, r + 1):
                    tot = tot + pad_s[sl, pl.ds(half + s + r0, rb), :]
                centre = pad_s[sl, pl.ds(pl.multiple_of(half + r0, SUBLANES), rb), :]
                parts.append((tot / cnt - centre).astype(BF16))
            y = jnp.dot(jnp.concatenate(parts, axis=1), w_ref[gi], preferred_element_type=F32)
            o_ref[pl.ds(r0, rb), cols] = (y * s_ref[:, cols]).astype(o_ref.dtype)
            return carry

        lax.fori_loop(0, n // rb, copy_rows, 0)
        lax.fori_loop(0, n // rb, pool_rows, 0, unroll=min(2, n // rb))


def _pool_call(px, col_block, pool_w, pool_scale):
    b, n, _ = px.shape
    g, gc, _ = pool_w.shape
    width = g * gc
    return pl.pallas_call(
        _pool_kernel,
        grid=(b,),
        in_specs=[
            pl.BlockSpec((None, n, width), lambda i: (i, 0, col_block)),
            pl.BlockSpec((g, gc, gc), lambda i: (0, 0, 0)),
            pl.BlockSpec((1, width), lambda i: (0, 0)),
        ],
        out_specs=pl.BlockSpec((None, n, width), lambda i: (i, 0, 0)),
        out_shape=jax.ShapeDtypeStruct((b, n, width), BF16),
        scratch_shapes=[pltpu.VMEM((gc // LANES, n + 2 * SUBLANES * pl.cdiv(max(POOL_RADII), SUBLANES), LANES), F32)],
        compiler_params=_params("arbitrary"),
        name="pool",
    )(px, pool_w, pool_scale.reshape(1, width))


def _sgu_kernel(u_ref, v_ref, lg_ref, lb_ref, ws_ref, bs_ref, o_ref, vn_ref):
    tr, wdt = v_ref.shape
    gw = wdt // SGU_GROUPS
    rc = 64

    def ln_body(r, carry):
        rows = pl.ds(pl.multiple_of(r * rc, rc), rc)
        v = v_ref[rows, :].astype(F32)
        mu = jnp.mean(v, axis=-1, keepdims=True)
        d = v - mu
        var = jnp.mean(d * d, axis=-1, keepdims=True)
        vn_ref[rows, :] = (d * lax.rsqrt(var + EPS) * lg_ref[...] + lb_ref[...]).astype(BF16)
        return carry

    lax.fori_loop(0, tr // rc, ln_body, 0)
    for ch in range(tr // SGU_CHUNK):
        rows = slice(ch * SGU_CHUNK, (ch + 1) * SGU_CHUNK)
        for g in range(SGU_GROUPS):
            cols = slice(g * gw, (g + 1) * gw)
            mixed = jnp.dot(ws_ref[g], vn_ref[rows, cols], preferred_element_type=F32) + bs_ref[:, g:g + 1]
            o_ref[rows, cols] = (u_ref[rows, cols].astype(F32) * mixed).astype(o_ref.dtype)


def _sgu_call(zz, ln_g, ln_b, ws, bs_t, tr):
    b, n, w2 = zz.shape
    wdt = w2 // 2
    return pl.pallas_call(
        _sgu_kernel,
        grid=(b, n // tr),
        in_specs=[
            pl.BlockSpec((None, tr, wdt), lambda i, r: (i, r, 0)),
            pl.BlockSpec((None, tr, wdt), lambda i, r: (i, r, 1)),
            pl.BlockSpec((1, wdt), lambda i, r: (0, 0)),
            pl.BlockSpec((1, wdt), lambda i, r: (0, 0)),
            pl.BlockSpec((SGU_GROUPS, SGU_CHUNK, SGU_CHUNK), lambda i, r: (0, 0, 0)),
            pl.BlockSpec((SGU_CHUNK, SGU_GROUPS), lambda i, r: (0, 0)),
        ],
        out_specs=pl.BlockSpec((None, tr, wdt), lambda i, r: (i, r, 0)),
        out_shape=jax.ShapeDtypeStruct((b, n, wdt), BF16),
        scratch_shapes=[pltpu.VMEM((tr, wdt), BF16)],
        compiler_params=_params("arbitrary", "arbitrary"),
        name="sgu",
    )(zz, zz, ln_g.reshape(1, wdt), ln_b.reshape(1, wdt), ws, bs_t)


def _grid_pos_embed(n, d):
    rows = n // GRID_W
    nf = d // 4
    omega = 1.0 / (10000.0 ** (jnp.arange(nf, dtype=F32) / nf))
    r = jnp.repeat(jnp.arange(rows, dtype=F32), GRID_W)
    col = jnp.tile(jnp.arange(GRID_W, dtype=F32), rows)
    ar = r[:, None] * omega
    ac = col[:, None] * omega
    return jnp.concatenate([jnp.sin(ar), jnp.cos(ar), jnp.sin(ac), jnp.cos(ac)], axis=-1)


def _pick_tile(n, candidates):
    for t in candidates:
        if n % t == 0:
            return t
    raise ValueError(f"no tile for {n}")


def _gate_rows(gcol, heads):
    b, n, _ = gcol.shape
    c = GDN_CHUNK
    g = gcol[:, :, :4 * heads].reshape(b, n // c, c, 2, 2, heads)
    g = jnp.transpose(g, (0, 5, 1, 3, 4, 2)).reshape(b, heads, n // c, 2, 2 * c)
    return jnp.pad(g, ((0, 0), (0, 0), (0, 0), (0, SUBLANES - 2), (0, 0)))


def _mlp(h, mods, layer, norm_g, w1, w2, final_g, n_seq, batch_row):
    tm = _pick_tile(n_seq, (1024, 512, 256, 128))
    hid = _nm_call(h, None, norm_g, mods, layer, 3, batch_row(tm), w1, act="relu2", out_dtype=BF16,
                   tm=tm, tn=_pick_tile(w1.shape[1], (2048, 1024, 512, 256, 128)), rows_per_seq=n_seq,
                   name=f"mlp_up{layer}")
    tm2 = _pick_tile(n_seq, (512, 256, 128))
    return _mr_call(hid, w2, h, None, mods, layer, 5, final_g, tm=tm2,
                    tk=_pick_tile(w2.shape[0], (2048, 1024, 512, 256, 128)), rows_per_seq=n_seq,
                    name=f"mlp_down{layer}")


def kernel(x, c, ctx, c_ctx, ada_w, ada_b, norm1_g, norm2_g, mlp_w1, mlp_w2, ev_w_in, ev_conv_w, ev_a_log,
           ev_dt_bias, ev_onorm_g, ev_pool_w, ev_pool_scale, ev_w_out, od_w_in, od_ln_g, od_ln_b, od_ws, od_bs,
           od_w_out, final_g):
    b, n_seq, d = x.shape
    n_ctx = ctx.shape[1]
    depth = ada_w.shape[0]
    heads = ev_a_log.shape[-1]
    qk_w = heads * HEAD_DIM
    gdn_qkv = 3 * qk_w
    gdn_in = gdn_qkv + 4 * heads
    pool_w_dim = ev_pool_scale.shape[-1]
    assert b + 1 <= MOD_ROWS and 4 * heads <= LANES and n_seq % GDN_CHUNK == 0 and n_ctx % GDN_CHUNK == 0
    assert qk_w % 1024 == 0 and pool_w_dim == qk_w

    cc = jnp.zeros((MOD_ROWS, d), F32).at[:b].set(c).at[b].set(c_ctx)
    mods = _ada_call(cc, ada_w, ada_b).reshape(depth, MOD_ROWS, N_ADA, 1, d)

    def batch_row(tm):
        return lambda i: (i * tm) // n_seq

    pos = _grid_pos_embed(n_seq, d)
    h = x.reshape(b * n_seq, d)
    last_even = 2 * ((depth - 1) // 2)
    assert last_even == 0, "context stream is only advanced through its first DeltaNet layer"

    for i in range(depth):
        j = i // 2
        fin = final_g if i == depth - 1 else None
        if i % 2 == 0:
            w_in = ev_w_in[j]
            ab_pad = LANES - 4 * heads
            n_cols = 5 * qk_w + LANES
            n_pad = -n_cols % 768
            w_x = jnp.concatenate([
                w_in[:, :gdn_qkv], w_in[:, gdn_in:gdn_in + 2 * qk_w], w_in[:, gdn_qkv:gdn_in],
                jnp.zeros((d, ab_pad + n_pad), F32)], axis=1).astype(BF16)
            w_c = jnp.concatenate([w_in[:, :gdn_in], jnp.zeros((d, ab_pad), F32)], axis=1).astype(BF16)
            blk = qk_w // LANES
            tm = _pick_tile(n_seq, (1024, 512, 256, 128))
            px, gate_x = _nm_call(h, pos, norm1_g[i], mods, i, 0, batch_row(tm), w_x, act="none", out_dtype=BF16,
                                  tm=tm, tn=768, rows_per_seq=n_seq, name="even_in", keep_col=5 * qk_w)
            tmc = _pick_tile(b * n_ctx, (1024, 512, 256, 128))
            pc, gate_c = _nm_call(ctx.reshape(b * n_ctx, d), None, norm1_g[i], mods, i, 0, lambda t: b, w_c,
                                  act="none", out_dtype=BF16, tm=tmc, tn=_pick_tile(w_c.shape[1], (640, 128)),
                                  rows_per_seq=b * n_ctx, name="even_in_ctx", keep_col=gdn_qkv)
            px = px.reshape(b, n_seq, -1)
            pc = pc.reshape(b, n_ctx, -1)

            lane_pad = jnp.zeros((LANES - 2 * heads,), F32)
            alog_row = jnp.concatenate([ev_a_log[j].reshape(-1), lane_pad]).reshape(1, LANES)
            dtb_row = jnp.concatenate([ev_dt_bias[j].reshape(-1), lane_pad]).reshape(1, LANES)
            gcol = jnp.concatenate([_gates_call(gate_c.reshape(b, n_ctx, LANES), 0, alog_row, dtb_row, heads),
                                    _gates_call(gate_x.reshape(b, n_seq, LANES), 0, alog_row, dtb_row, heads)],
                                   axis=1)
            y_a = _gdn_call(px, pc, gcol, _gate_rows(gcol, heads), ev_conv_w[j], ev_onorm_g[j],
                            heads, 0, blk, 2 * blk, 3 * blk)
            y_b = _pool_call(px, 4 * qk_w // pool_w_dim, ev_pool_w[j].astype(BF16), ev_pool_scale[j])
            y = jnp.concatenate([y_a, y_b], axis=-1).reshape(b * n_seq, -1)
            w_out = ev_w_out[j].astype(BF16)
        else:
            tm = _pick_tile(n_seq, (1024, 512, 256, 128))
            w_od = od_w_in[j].astype(BF16)
            zz = _nm_call(h, None, norm1_g[i], mods, i, 0, batch_row(tm), w_od, act="gelu", out_dtype=BF16,
                          tm=tm, tn=_pick_tile(w_od.shape[1], (2048, 1024, 512, 256, 128)), rows_per_seq=n_seq,
                          name="odd_in")
            y = _sgu_call(zz.reshape(b, n_seq, -1), od_ln_g[j], od_ln_b[j], od_ws[j].astype(BF16),
                          jnp.transpose(od_bs[j]), _pick_tile(n_seq, (512, 256, 128)))
            y = y.reshape(b * n_seq, -1)
            w_out = od_w_out[j].astype(BF16)
        tm2 = _pick_tile(n_seq, (512, 256, 128))
        h = _mr_call(y, w_out, h, pos, mods, i, 2, None, tm=tm2,
                     tk=_pick_tile(y.shape[1], (2048, 1024, 512, 256, 128)), rows_per_seq=n_seq, name=f"mix_out{i}")
        pos = None
        h = _mlp(h, mods, i, norm2_g[i], mlp_w1[i].astype(BF16), mlp_w2[i].astype(BF16), fin, n_seq, batch_row)
    return h.reshape(b, n_seq, d)
```

```python
import functools
import math

import jax
import jax.numpy as jnp
import numpy as np
from jax import lax
from jax.experimental import pallas as pl
from jax.experimental.pallas import tpu as pltpu

F32 = jnp.float32
BF16 = jnp.bfloat16

EPS = 1e-6
GRID_W = 64
N_ADA = 6
HEAD_DIM = 128
GDN_CHUNK = 64
SHORT_CONV = 5
POOL_RADII = (1, 2, 4, 8)
SGU_GROUPS = 4
SGU_CHUNK = 128

LANES = 128
SUBLANES = 8
MOD_ROWS = 32
VMEM_LIMIT = 56 * 1024 * 1024

GDN_HEADS_PER_STEP = 2
GDN_CHUNK_GROUP = 9


def _params(*sem):
    return pltpu.CompilerParams(dimension_semantics=sem, vmem_limit_bytes=VMEM_LIMIT)


def _silu(x):
    return x * jax.nn.sigmoid(x)


def _cast_kernel(x_ref, o_ref):
    o_ref[...] = x_ref[...].astype(o_ref.dtype)


def _cast_call(w, layer):
    _, k, n = w.shape
    tr = _pick_tile(k, [t for t in (2048, 1024, 512, 256, 128, 64, 32, 16) if t * n * 4 <= 8 * 1024 * 1024])
    return pl.pallas_call(
        _cast_kernel,
        grid=(k // tr,),
        in_specs=[pl.BlockSpec((None, tr, n), lambda i: (layer, i, 0))],
        out_specs=pl.BlockSpec((tr, n), lambda i: (i, 0)),
        out_shape=jax.ShapeDtypeStruct((k, n), BF16),
        compiler_params=_params("arbitrary"),
        name="cast_weight",
    )(w)


def _ada_kernel(c_ref, w_ref, b_ref, o_ref):
    s = _silu(c_ref[...]).astype(BF16)
    o_ref[...] = jnp.dot(s, w_ref[...].astype(BF16), preferred_element_type=F32) + b_ref[...]


def _ada_call(cc, ada_w, ada_b):
    depth, d, n = ada_w.shape
    tn = 1024
    return pl.pallas_call(
        _ada_kernel,
        grid=(depth, n // tn),
        in_specs=[
            pl.BlockSpec((MOD_ROWS, d), lambda l, j: (0, 0)),
            pl.BlockSpec((None, d, tn), lambda l, j: (l, 0, j)),
            pl.BlockSpec((None, 1, tn), lambda l, j: (l, 0, j)),
        ],
        out_specs=pl.BlockSpec((None, MOD_ROWS, tn), lambda l, j: (l, 0, j)),
        out_shape=jax.ShapeDtypeStruct((depth, MOD_ROWS, n), F32),
        compiler_params=_params("arbitrary", "arbitrary"),
        name="ada",
    )(cc, ada_w, ada_b.reshape(depth, 1, n))


def _act(y, act):
    if act == "relu2":
        r = jnp.maximum(y, 0.0)
        return r * r
    if act == "gelu":
        return 0.5 * y * (1.0 + lax.erf(y * math.sqrt(0.5)))
    return y


def _nm_kernel(*refs, has_pos, act, row_chunk, keep):
    x_ref = refs[0]
    rest = list(refs[1:])
    pos_ref = rest.pop(0) if has_pos else None
    g_ref, sh_ref, sc_ref, w_ref, o_ref = rest[:5]
    keep_ref = rest[5] if keep is not None else None
    xn_ref = rest[-1]
    tm = x_ref.shape[0]

    @pl.when(pl.program_id(1) == 0)
    def _():
        gm = g_ref[...] * (1.0 + sc_ref[...])
        shift = sh_ref[...]

        def body(r, carry):
            rows = pl.ds(pl.multiple_of(r * row_chunk, row_chunk), row_chunk)
            x = x_ref[rows, :]
            if has_pos:
                x = x + pos_ref[rows, :]
            ms = jnp.mean(x * x, axis=-1, keepdims=True)
            xn_ref[rows, :] = (x * lax.rsqrt(ms + EPS) * gm + shift).astype(BF16)
            return carry

        lax.fori_loop(0, tm // row_chunk, body, 0, unroll=8)

    acc = jnp.dot(xn_ref[...], w_ref[...], preferred_element_type=F32)
    o_ref[...] = _act(acc, act).astype(o_ref.dtype)
    if keep is not None:
        keep_tile, keep_off = keep

        @pl.when(pl.program_id(1) == keep_tile)
        def _():
            keep_ref[...] = acc[:, keep_off:keep_off + LANES]


def _seq_major(i, tiles_per_seq, n_seqs):
    seq = i % n_seqs
    tile = i // n_seqs
    return seq * tiles_per_seq + tile, seq, tile


def _nm_call(x, pos, gain, mods, layer, k_shift, mod_row, w, *, act, out_dtype, tm, tn, rows_per_seq, name,
             keep_col=None):
    m, d = x.shape
    n = w.shape[1]
    has_pos = pos is not None
    tiles_per_seq = rows_per_seq // tm
    n_seqs = m // rows_per_seq

    def row_tile(i):
        return _seq_major(i, tiles_per_seq, n_seqs)[0] if has_pos else i

    in_specs = [pl.BlockSpec((tm, d), lambda i, j: (row_tile(i), 0))]
    args = [x]
    if has_pos:
        in_specs.append(pl.BlockSpec((tm, d), lambda i, j: (_seq_major(i, tiles_per_seq, n_seqs)[2], 0)))
        args.append(pos)
    in_specs += [
        pl.BlockSpec((1, d), lambda i, j: (0, 0)),
        pl.BlockSpec((None, None, None, 1, d), lambda i, j: (layer, mod_row(row_tile(i)), k_shift, 0, 0)),
        pl.BlockSpec((None, None, None, 1, d), lambda i, j: (layer, mod_row(row_tile(i)), k_shift + 1, 0, 0)),
        pl.BlockSpec((d, tn), lambda i, j: (0, j)),
    ]
    args += [gain.reshape(1, d), mods, mods, w]
    out_specs = [pl.BlockSpec((tm, tn), lambda i, j: (row_tile(i), j))]
    out_shape = [jax.ShapeDtypeStruct((m, n), out_dtype)]
    keep = None
    if keep_col is not None:
        keep = (keep_col // tn, keep_col % tn)
        assert keep[1] + LANES <= tn
        out_specs.append(pl.BlockSpec((tm, LANES), lambda i, j: (row_tile(i), 0)))
        out_shape.append(jax.ShapeDtypeStruct((m, LANES), F32))
    res = pl.pallas_call(
        functools.partial(_nm_kernel, has_pos=has_pos, act=act, row_chunk=16, keep=keep),
        grid=(m // tm, n // tn),
        in_specs=in_specs,
        out_specs=out_specs,
        out_shape=out_shape,
        scratch_shapes=[pltpu.VMEM((tm, d), BF16)],
        compiler_params=_params("arbitrary", "arbitrary"),
        name=name,
    )(*args)
    return res if keep_col is not None else res[0]


def _mr_kernel(*refs, has_pos, has_final, k_steps):
    a_ref, w_ref, h_ref, gate_ref = refs[:4]
    rest = list(refs[4:])
    pos_ref = rest.pop(0) if has_pos else None
    fg_ref = rest.pop(0) if has_final else None
    o_ref, *acc = rest

    def finish(total):
        h = h_ref[...]
        if has_pos:
            h = h + pos_ref[...]
        y = h + gate_ref[...] * total
        if has_final:
            ms = jnp.mean(y * y, axis=-1, keepdims=True)
            y = y * lax.rsqrt(ms + EPS) * fg_ref[...]
        o_ref[...] = y

    if k_steps == 1:
        finish(jnp.dot(a_ref[...], w_ref[...], preferred_element_type=F32))
        return
    acc_ref, = acc
    k = pl.program_id(1)

    @pl.when(k == 0)
    def _():
        acc_ref[...] = jnp.zeros_like(acc_ref)

    acc_ref[...] += jnp.dot(a_ref[...], w_ref[...], preferred_element_type=F32)

    @pl.when(k == k_steps - 1)
    def _():
        finish(acc_ref[...])


def _mr_call(a, w, h, pos, mods, layer, k_gate, final_g, *, tm, tk, rows_per_seq, name):
    m, kdim = a.shape
    d = w.shape[1]
    has_final = final_g is not None
    has_pos = pos is not None
    tiles_per_seq = rows_per_seq // tm
    n_seqs = m // rows_per_seq

    def row_tile(i):
        return _seq_major(i, tiles_per_seq, n_seqs)[0] if has_pos else i

    in_specs = [
        pl.BlockSpec((tm, tk), lambda i, k: (row_tile(i), k)),
        pl.BlockSpec((tk, d), lambda i, k: (k, 0)),
        pl.BlockSpec((tm, d), lambda i, k: (row_tile(i), 0)),
        pl.BlockSpec((None, None, None, 1, d), lambda i, k: (layer, row_tile(i) // tiles_per_seq, k_gate, 0, 0)),
    ]
    args = [a, w, h, mods]
    if has_pos:
        in_specs.append(pl.BlockSpec((tm, d), lambda i, k: (_seq_major(i, tiles_per_seq, n_seqs)[2], 0)))
        args.append(pos)
    if has_final:
        in_specs.append(pl.BlockSpec((1, d), lambda i, k: (0, 0)))
        args.append(final_g.reshape(1, d))
    k_steps = kdim // tk
    return pl.pallas_call(
        functools.partial(_mr_kernel, has_pos=has_pos, has_final=has_final, k_steps=k_steps),
        grid=(m // tm, k_steps),
        in_specs=in_specs,
        out_specs=pl.BlockSpec((tm, d), lambda i, k: (row_tile(i), 0)),
        out_shape=jax.ShapeDtypeStruct((m, d), F32),
        scratch_shapes=[pltpu.VMEM((tm, d), F32)] if k_steps > 1 else [],
        compiler_params=_params("arbitrary", "arbitrary"),
        name=name,
    )(*args)


def _gates_kernel(ab_ref, alog_ref, dtb_ref, o_ref, *, heads):
    x = ab_ref[...]
    n = x.shape[0]
    lane = lax.broadcasted_iota(jnp.int32, x.shape, 1)
    row = lax.broadcasted_iota(jnp.int32, x.shape, 0)
    pos = row % GDN_CHUNK
    g = -jnp.exp(alog_ref[...]) * jax.nn.softplus(x + dtb_ref[...])
    pre = g
    suf = g
    s = 1
    while s < GDN_CHUNK:
        pre = pre + jnp.where(pos >= s, pltpu.roll(pre, s, 0), 0.0)
        suf = suf + jnp.where(pos < GDN_CHUNK - s, pltpu.roll(suf, n - s, 0), 0.0)
        s *= 2
    beta = jax.nn.sigmoid(x)
    o_ref[...] = jnp.where(lane < heads, pre, jnp.where(lane < 2 * heads, suf, beta))


def _gates_call(p, col_block, alog_row, dtb_row, heads):
    b, n, _ = p.shape
    return pl.pallas_call(
        functools.partial(_gates_kernel, heads=heads),
        grid=(b,),
        in_specs=[
            pl.BlockSpec((None, n, LANES), lambda i: (i, 0, col_block)),
            pl.BlockSpec((1, LANES), lambda i: (0, 0)),
            pl.BlockSpec((1, LANES), lambda i: (0, 0)),
        ],
        out_specs=pl.BlockSpec((None, n, LANES), lambda i: (i, 0, 0)),
        out_shape=jax.ShapeDtypeStruct((b, n, LANES), F32),
        compiler_params=_params("arbitrary"),
        name="gdn_gates",
    )(p, alog_row, dtb_row)


def _l2norm(t):
    return t * lax.rsqrt(jnp.sum(t * t, axis=-1, keepdims=True) + EPS)


def _dot_nt(a, b):
    return lax.dot_general(a, b, (((1,), (1,)), ((), ())), preferred_element_type=F32)


def _dot_tn(a, b):
    return lax.dot_general(a, b, (((0,), (0,)), ((), ())), preferred_element_type=F32)


def _dot(a, b):
    return jnp.dot(a, b, preferred_element_type=F32)


def _tri_masks():
    c = GDN_CHUNK
    r = np.arange(c)[:, None]
    col = np.arange(2 * c)[None, :]
    j = col % c
    bwd = col >= c
    hi = np.where(bwd, j, r)
    lo = np.where(bwd, r, j)
    masks = [hi >= lo, hi > lo, r == j]
    s = 1
    while s < c:
        masks.append(((r ^ j) < 2 * s) & ((hi & s) != 0) & ((lo & s) == 0))
        s *= 2
    return np.stack(masks).astype(np.float32)


def _twice_diag(x16):
    z = jnp.zeros_like(x16)
    return jnp.concatenate([jnp.concatenate([x16, z], axis=1), jnp.concatenate([z, x16], axis=1)], axis=0)


def _gdn_kernel(qx_ref, kx_ref, vx_ref, zx_ref, qc_ref, kc_ref, vc_ref, gcol_ref, grow_ref, mask_ref, m16_ref,
                wq_ref, wk_ref, wv_ref, og_ref, y_ref,
                q_s, k_s, v_s, u_s, w_s, qd_s, kd_s, qkm_s, dl_s, o_s, st_s, pad_s,
                *, heads, hg, chunk_group, n_units):
    t_step = pl.program_id(0)
    groups = heads // hg
    hblk = jnp.minimum(t_step, n_units - 1) % groups
    slot_w = t_step % 2
    slot_s = 1 - slot_w
    n_x = qx_ref.shape[0]
    n_c = qc_ref.shape[0]
    c = GDN_CHUNK
    nc_c, nc_x = n_c // c, n_x // c
    n_chunks = nc_c + nc_x
    n_tot = n_c + n_x
    n_levels = mask_ref.shape[0] - 3
    halo = SUBLANES

    def stage1():
        pad_s[0:halo, :] = jnp.zeros((halo, LANES), F32)
        for hh in range(hg):
            lanes = slice(hh * LANES, (hh + 1) * LANES)
            for src_c, src_x, dst, w_ref, kind in ((qc_ref, qx_ref, q_s, wq_ref, "q"),
                                                   (kc_ref, kx_ref, k_s, wk_ref, "k"),
                                                   (vc_ref, vx_ref, v_s, wv_ref, "v")):
                w = w_ref[:, lanes]
                for src, base, n in ((src_c, 0, n_c), (src_x, n_c, n_x)):
                    rb = math.gcd(n, 256)

                    def copy_rows(i, carry, src=src, lanes=lanes, rb=rb):
                        r0 = pl.multiple_of(i * rb, rb)
                        pad_s[pl.ds(pl.multiple_of(halo + r0, halo), rb), :] = src[pl.ds(r0, rb), lanes].astype(F32)
                        return carry

                    def conv_rows(i, carry, base=base, w=w, kind=kind, dst=dst, hh=hh, rb=rb):
                        r0 = pl.multiple_of(i * rb, rb)
                        first = halo - (SHORT_CONV - 1) // 2 + r0
                        t = w[0:1, :] * pad_s[pl.ds(first, rb), :]
                        for tap in range(1, SHORT_CONV):
                            t = t + w[tap:tap + 1, :] * pad_s[pl.ds(first + tap, rb), :]
                        t = _silu(t)
                        if kind == "q":
                            t = _l2norm(t) * (HEAD_DIM ** -0.5)
                        elif kind == "k":
                            t = _l2norm(t)
                        dst[hh, pl.ds(pl.multiple_of(base + r0, halo), rb), :] = t
                        return carry

                    lax.fori_loop(0, n // rb, copy_rows, 0)
                    pad_s[halo + n:2 * halo + n, :] = jnp.zeros((halo, LANES), F32)
                    lax.fori_loop(0, n // rb, conv_rows, 0, unroll=min(4, n // rb))

    glane = lax.broadcasted_iota(jnp.int32, (c, LANES), 1)
    left = glane < c

    def block_diag(x):
        x16 = x.astype(BF16)
        return jnp.concatenate([x16 * m16_ref[0], x16 * m16_ref[1]], axis=0)

    def wy_group(it):
        items = []
        for hh in range(hg):
            for j in range(chunk_group):
                ch = it * chunk_group + j
                rows = pl.ds(pl.multiple_of(ch * c, c), c)
                head = hblk * hg + hh
                blk = gcol_ref[rows, :]

                def col(idx, blk=blk):
                    return jnp.broadcast_to(jnp.sum(jnp.where(glane == idx, blk, 0.0), axis=-1, keepdims=True), (c, LANES))

                k = k_s[hh, rows, :]
                k16 = k.astype(BF16)
                items.append(dict(hh=hh, ch=ch, rows=rows, k=k, k16=k16, q=q_s[hh, rows, :], v=v_s[hh, rows, :],
                                  gf=col(head), gb=col(heads + head), bf=col(2 * heads + head),
                                  bb=col(3 * heads + head), rp=grow_ref[hh, ch]))
        raws = [_dot_nt(jnp.concatenate([d["k16"], d["q"].astype(BF16)], axis=0),
                        jnp.concatenate([d["k16"], d["k16"]], axis=0)) for d in items]
        yield
        for d, raw in zip(items, raws):
            gam_col = jnp.where(left, d["gf"], d["gb"])
            beta_col = jnp.where(left, d["bf"], d["bb"])
            incl = mask_ref[0]
            diff = gam_col - d["rp"][0:1, :]
            decay = jnp.exp(jnp.where(incl > 0.0, diff, 0.0)) * incl
            a = raw[:c] * decay * beta_col * mask_ref[1]
            d["a16"] = a.astype(BF16)
            qkm_s[slot_w, d["hh"], d["ch"]] = (raw[c:] * decay).astype(BF16)
            d["t"] = mask_ref[2] - a * mask_ref[3]
        for lvl in range(1, n_levels):
            xs = [_dot(d["a16"] * m16_ref[2 + lvl], block_diag(d["t"])) for d in items]
            yield
            ys = [_dot(d["t"].astype(BF16), block_diag(x)) for d, x in zip(items, xs)]
            yield
            for d, y in zip(items, ys):
                d["t"] = d["t"] - y
        us, ws = [], []
        for d in items:
            beta_row = d["rp"][1:2, :]
            scale_w = beta_row * jnp.exp(d["rp"][0:1, :])
            us.append(_dot((d["t"] * beta_row).astype(BF16), _twice_diag(d["v"].astype(BF16))))
            ws.append(_dot((d["t"] * scale_w).astype(BF16), _twice_diag(d["k16"])))
        yield
        for d, u, w in zip(items, us, ws):
            hh, ch, rows = d["hh"], d["ch"], d["rows"]
            for dirn, gam in ((0, d["gf"]), (1, d["gb"])):
                lanes = slice(dirn * LANES, (dirn + 1) * LANES)
                g_last = gam[c - 1:c, :] if dirn == 0 else gam[0:1, :]
                u_s[slot_w, hh, dirn, rows, :] = u[:, lanes].astype(BF16)
                w_s[slot_w, hh, dirn, rows, :] = w[:, lanes].astype(BF16)
                qd_s[slot_w, hh, dirn, rows, :] = (d["q"] * jnp.exp(gam)).astype(BF16)
                kd_s[slot_w, hh, dirn, rows, :] = (d["k"] * jnp.exp(g_last - gam)).astype(BF16)
                dl_s[slot_w, hh, dirn, ch] = jnp.broadcast_to(jnp.exp(g_last), (SUBLANES, LANES))

    chains = [(hh, dirn) for hh in range(hg) for dirn in range(2)]

    def scan_step(s, states):
        ch_b = jnp.where(s < nc_c, nc_c - 1 - s, n_chunks + nc_c - 1 - s)
        chunk = [s if dirn == 0 else ch_b for hh, dirn in chains]
        rows = [pl.ds(pl.multiple_of(ch * c, c), c) for ch in chunk]
        s16 = [st.astype(BF16) for st in states]
        r1 = [_dot(jnp.concatenate([w_s[slot_s, hh, dirn, rw, :], qd_s[slot_s, hh, dirn, rw, :]], axis=0), st)
              for (hh, dirn), rw, st in zip(chains, rows, s16)]
        yield
        vn16 = [(u_s[slot_s, hh, dirn, rw, :].astype(F32) - r[:c]).astype(BF16)
                for (hh, dirn), rw, r in zip(chains, rows, r1)]
        upd = [_dot_tn(kd_s[slot_s, hh, dirn, rw, :], vn) for (hh, dirn), rw, vn in zip(chains, rows, vn16)]
        zero = jnp.zeros((c, LANES), BF16)
        intra = [_dot(qkm_s[slot_s, hh, ch], jnp.concatenate([vn, zero] if dirn == 0 else [zero, vn], axis=0))
                 for (hh, dirn), ch, vn in zip(chains, chunk, vn16)]
        yield
        for (hh, dirn), rw, r, o in zip(chains, rows, r1, intra):
            o_s[hh, rw, :] += r[c:] + o
        states[:] = [st * dl_s[slot_s, hh, dirn, ch][0:1, :] + up
                     for (hh, dirn), ch, st, up in zip(chains, chunk, states, upd)]

    def scan_group(it):
        states = [st_s[i] for i in range(len(chains))]
        for j in range(chunk_group):
            yield from scan_step(it * chunk_group + j, states)
        for i, st in enumerate(states):
            st_s[i] = st

    def interleave(*gens):
        live = [[gen, 0, n_stages] for gen, n_stages in gens]
        while live:
            live.sort(key=lambda e: e[1] / e[2])
            entry = live[0]
            try:
                next(entry[0])
                entry[1] += 1
            except StopIteration:
                live.remove(entry)

    def scan_init():
        for i in range(len(chains)):
            st_s[i] = jnp.zeros((HEAD_DIM, HEAD_DIM), F32)
        zrows = math.gcd(n_tot, 256)

        def zero_rows(i, carry):
            for hh in range(hg):
                o_s[hh, pl.ds(pl.multiple_of(i * zrows, zrows), zrows), :] = jnp.zeros((zrows, LANES), F32)
            return carry

        lax.fori_loop(0, n_tot // zrows, zero_rows, 0)

    def stage4():
        rb = math.gcd(n_x, 256)
        for hh in range(hg):
            lanes = slice(hh * LANES, (hh + 1) * LANES)

            def gate_rows(i, carry, hh=hh, lanes=lanes):
                rows = pl.ds(pl.multiple_of(i * rb, rb), rb)
                o = o_s[hh, pl.ds(pl.multiple_of(n_c + i * rb, math.gcd(n_c, rb)), rb), :]
                ms = jnp.mean(o * o, axis=-1, keepdims=True)
                y = o * lax.rsqrt(ms + EPS) * og_ref[...]
                y_ref[rows, lanes] = (y * _silu(zx_ref[rows, lanes].astype(F32))).astype(y_ref.dtype)
                return carry

            lax.fori_loop(0, n_x // rb, gate_rows, 0, unroll=min(4, n_x // rb))

    def run(do_wy, do_scan):
        if do_wy:
            stage1()
        if do_scan:
            scan_init()

        def group(it, carry):
            gens = []
            if do_wy:
                gens.append((wy_group(it), 2 * n_levels + 1))
            if do_scan:
                gens.append((scan_group(it), 2 * chunk_group + 1))
            interleave(*gens)
            return carry

        lax.fori_loop(0, n_chunks // chunk_group, group, 0)
        if do_scan:
            stage4()

    @pl.when(t_step == 0)
    def _():
        run(True, False)

    @pl.when(jnp.logical_and(t_step > 0, t_step < n_units))
    def _():
        run(True, True)

    @pl.when(t_step == n_units)
    def _():
        run(False, True)


def _gdn_call(px, pc, gcol, grow, conv_w, onorm_g, heads, blk_q, blk_k, blk_v, blk_z):
    b, n_x, _ = px.shape
    n_c = pc.shape[1]
    c = GDN_CHUNK
    n_tot = n_c + n_x
    n_chunks = n_tot // c
    hg = GDN_HEADS_PER_STEP if heads % GDN_HEADS_PER_STEP == 0 else 1
    chunk_group = max(g for g in range(1, GDN_CHUNK_GROUP + 1) if n_chunks % g == 0)
    assert all(blk % hg == 0 for blk in (blk_q, blk_k, blk_v, blk_z))
    wide = hg * LANES
    groups = heads // hg
    n_units = b * groups
    tri = _tri_masks()
    masks = jnp.asarray(tri)
    lane_half = np.arange(2 * c)[None, :] < c
    half_masks = np.stack([np.broadcast_to(lane_half, (c, 2 * c)), np.broadcast_to(~lane_half, (c, 2 * c))])
    masks16 = jnp.asarray(np.concatenate([half_masks.astype(np.float32), tri[3:]]), dtype=BF16)

    def build(t):
        u = jnp.minimum(t, n_units - 1)
        return u // groups, u % groups

    def finish(t):
        u = jnp.maximum(t - 1, 0)
        return u // groups, u % groups

    def seq_spec(n, blk0, which):
        return pl.BlockSpec((None, n, wide), lambda t: (which(t)[0], 0, blk0 // hg + which(t)[1]))

    def wspec(blk0):
        return pl.BlockSpec((SHORT_CONV, wide), lambda t: (0, blk0 // hg + build(t)[1]))

    in_specs = [
        seq_spec(n_x, blk_q, build), seq_spec(n_x, blk_k, build), seq_spec(n_x, blk_v, build),
        seq_spec(n_x, blk_z, finish),
        seq_spec(n_c, blk_q, build), seq_spec(n_c, blk_k, build), seq_spec(n_c, blk_v, build),
        pl.BlockSpec((None, n_tot, LANES), lambda t: (build(t)[0], 0, 0)),
        pl.BlockSpec((None, hg, n_chunks, SUBLANES, LANES), lambda t: (build(t)[0], build(t)[1], 0, 0, 0)),
        pl.BlockSpec(masks.shape, lambda t: (0, 0, 0)),
        pl.BlockSpec(masks16.shape, lambda t: (0, 0, 0)),
        wspec(0), wspec(heads), wspec(2 * heads),
        pl.BlockSpec((1, LANES), lambda t: (0, 0)),
    ]
    per_dir = pltpu.VMEM((2, hg, 2, n_tot, LANES), BF16)
    scratch = [pltpu.VMEM((hg, n_tot, LANES), F32)] * 3 + [
        per_dir, per_dir, per_dir, per_dir,
        pltpu.VMEM((2, hg, n_chunks, c, 2 * c), BF16),
        pltpu.VMEM((2, hg, 2, n_chunks, SUBLANES, LANES), F32),
        pltpu.VMEM((hg, n_tot, LANES), F32),
        pltpu.VMEM((2 * hg, HEAD_DIM, HEAD_DIM), F32),
        pltpu.VMEM((max(n_x, n_c) + 2 * SUBLANES, LANES), F32),
    ]
    return pl.pallas_call(
        functools.partial(_gdn_kernel, heads=heads, hg=hg, chunk_group=chunk_group, n_units=n_units),
        grid=(n_units + 1,),
        in_specs=in_specs,
        out_specs=pl.BlockSpec((None, n_x, wide), lambda t: (finish(t)[0], 0, finish(t)[1])),
        out_shape=jax.ShapeDtypeStruct((b, n_x, heads * HEAD_DIM), BF16),
        scratch_shapes=scratch,
        compiler_params=_params("arbitrary"),
        name="gdn",
    )(px, px, px, px, pc, pc, pc, gcol, grow, masks, masks16, conv_w, conv_w, conv_w, onorm_g.reshape(1, LANES))


def _pool_kernel(p_ref, w_ref, s_ref, o_ref, pad_s):
    n = p_ref.shape[0]
    gc = w_ref.shape[1]
    slabs = pad_s.shape[0]
    halo = pad_s.shape[1] - n
    half = halo // 2
    for sl in range(slabs):
        pad_s[sl, 0:half, :] = jnp.zeros((half, LANES), F32)
        pad_s[sl, half + n:halo + n, :] = jnp.zeros((half, LANES), F32)
    rb = math.gcd(n, 256)
    for gi, r in enumerate(POOL_RADII):
        cols = slice(gi * gc, (gi + 1) * gc)

        def copy_rows(i, carry):
            r0 = pl.multiple_of(i * rb, rb)
            for sl in range(slabs):
                lanes = slice(gi * gc + sl * LANES, gi * gc + (sl + 1) * LANES)
                pad_s[sl, pl.ds(pl.multiple_of(half + r0, SUBLANES), rb), :] = p_ref[pl.ds(r0, rb), lanes].astype(F32)
            return carry

        def pool_rows(i, carry):
            r0 = pl.multiple_of(i * rb, rb)
            tr = lax.broadcasted_iota(jnp.int32, (rb, 1), 0) + r0
            cnt = (jnp.minimum(tr + r + 1, n) - jnp.maximum(tr - r, 0)).astype(F32)
            parts = []
            for sl in range(slabs):
                tot = pad_s[sl, pl.ds(half - r + r0, rb), :]
                for s in range(1 - r, r + 1):
                    tot = tot + pad_s[sl, pl.ds(half + s + r0, rb), :]
                centre = pad_s[sl, pl.ds(pl.multiple_of(half + r0, SUBLANES), rb), :]
                parts.append((tot / cnt - centre).astype(BF16))
            y = jnp.dot(jnp.concatenate(parts, axis=1), w_ref[gi], preferred_element_type=F32)
            o_ref[pl.ds(r0, rb), cols] = (y * s_ref[:, cols]).astype(o_ref.dtype)
            return carry

        lax.fori_loop(0, n // rb, copy_rows, 0)
        lax.fori_loop(0, n // rb, pool_rows, 0, unroll=min(2, n // rb))


def _pool_call(px, col_block, pool_w, pool_scale):
    b, n, _ = px.shape
    g, gc, _ = pool_w.shape
    width = g * gc
    return pl.pallas_call(
        _pool_kernel,
        grid=(b,),
        in_specs=[
            pl.BlockSpec((None, n, width), lambda i: (i, 0, col_block)),
            pl.BlockSpec((g, gc, gc), lambda i: (0, 0, 0)),
            pl.BlockSpec((1, width), lambda i: (0, 0)),
        ],
        out_specs=pl.BlockSpec((None, n, width), lambda i: (i, 0, 0)),
        out_shape=jax.ShapeDtypeStruct((b, n, width), BF16),
        scratch_shapes=[pltpu.VMEM((gc // LANES, n + 2 * SUBLANES * pl.cdiv(max(POOL_RADII), SUBLANES), LANES), F32)],
        compiler_params=_params("arbitrary"),
        name="pool",
    )(px, pool_w, pool_scale.reshape(1, width))


def _sgu_kernel(u_ref, v_ref, lg_ref, lb_ref, ws_ref, bs_ref, o_ref, vn_ref):
    tr, wdt = v_ref.shape
    gw = wdt // SGU_GROUPS
    rc = 64

    def ln_body(r, carry):
        rows = pl.ds(pl.multiple_of(r * rc, rc), rc)
        v = v_ref[rows, :].astype(F32)
        mu = jnp.mean(v, axis=-1, keepdims=True)
        d = v - mu
        var = jnp.mean(d * d, axis=-1, keepdims=True)
        vn_ref[rows, :] = (d * lax.rsqrt(var + EPS) * lg_ref[...] + lb_ref[...]).astype(BF16)
        return carry

    lax.fori_loop(0, tr // rc, ln_body, 0)
    for ch in range(tr // SGU_CHUNK):
        rows = slice(ch * SGU_CHUNK, (ch + 1) * SGU_CHUNK)
        for g in range(SGU_GROUPS):
            cols = slice(g * gw, (g + 1) * gw)
            mixed = jnp.dot(ws_ref[g], vn_ref[rows, cols], preferred_element_type=F32) + bs_ref[:, g:g + 1]
            o_ref[rows, cols] = (u_ref[rows, cols].astype(F32) * mixed).astype(o_ref.dtype)


def _sgu_call(zz, ln_g, ln_b, ws, bs_t, tr):
    b, n, w2 = zz.shape
    wdt = w2 // 2
    return pl.pallas_call(
        _sgu_kernel,
        grid=(b, n // tr),
        in_specs=[
            pl.BlockSpec((None, tr, wdt), lambda i, r: (i, r, 0)),
            pl.BlockSpec((None, tr, wdt), lambda i, r: (i, r, 1)),
            pl.BlockSpec((1, wdt), lambda i, r: (0, 0)),
            pl.BlockSpec((1, wdt), lambda i, r: (0, 0)),
            pl.BlockSpec((SGU_GROUPS, SGU_CHUNK, SGU_CHUNK), lambda i, r: (0, 0, 0)),
            pl.BlockSpec((SGU_CHUNK, SGU_GROUPS), lambda i, r: (0, 0)),
        ],
        out_specs=pl.BlockSpec((None, tr, wdt), lambda i, r: (i, r, 0)),
        out_shape=jax.ShapeDtypeStruct((b, n, wdt), BF16),
        scratch_shapes=[pltpu.VMEM((tr, wdt), BF16)],
        compiler_params=_params("arbitrary", "arbitrary"),
        name="sgu",
    )(zz, zz, ln_g.reshape(1, wdt), ln_b.reshape(1, wdt), ws, bs_t)


def _grid_pos_embed(n, d):
    rows = n // GRID_W
    nf = d // 4
    omega = 1.0 / (10000.0 ** (jnp.arange(nf, dtype=F32) / nf))
    r = jnp.repeat(jnp.arange(rows, dtype=F32), GRID_W)
    col = jnp.tile(jnp.arange(GRID_W, dtype=F32), rows)
    ar = r[:, None] * omega
    ac = col[:, None] * omega
    return jnp.concatenate([jnp.sin(ar), jnp.cos(ar), jnp.sin(ac), jnp.cos(ac)], axis=-1)


def _pick_tile(n, candidates):
    for t in candidates:
        if n % t == 0:
            return t
    raise ValueError(f"no tile for {n}")


def _gate_rows(gcol, heads):
    b, n, _ = gcol.shape
    c = GDN_CHUNK
    g = gcol[:, :, :4 * heads].reshape(b, n // c, c, 2, 2, heads)
    g = jnp.transpose(g, (0, 5, 1, 3, 4, 2)).reshape(b, heads, n // c, 2, 2 * c)
    return jnp.pad(g, ((0, 0), (0, 0), (0, 0), (0, SUBLANES - 2), (0, 0)))


def _mlp(h, mods, layer, norm_g, w1, w2, final_g, n_seq, batch_row):
    tm = _pick_tile(n_seq, (1024, 512, 256, 128))
    hid = _nm_call(h, None, norm_g, mods, layer, 3, batch_row(tm), w1, act="relu2", out_dtype=BF16,
                   tm=tm, tn=_pick_tile(w1.shape[1], (2048, 1024, 512, 256, 128)), rows_per_seq=n_seq,
                   name=f"mlp_up{layer}")
    tm2 = _pick_tile(n_seq, (512, 256, 128))
    return _mr_call(hid, w2, h, None, mods, layer, 5, final_g, tm=tm2,
                    tk=_pick_tile(w2.shape[0], (2048, 1024, 512, 256, 128)), rows_per_seq=n_seq,
                    name=f"mlp_down{layer}")


def kernel(x, c, ctx, c_ctx, ada_w, ada_b, norm1_g, norm2_g, mlp_w1, mlp_w2, ev_w_in, ev_conv_w, ev_a_log,
           ev_dt_bias, ev_onorm_g, ev_pool_w, ev_pool_scale, ev_w_out, od_w_in, od_ln_g, od_ln_b, od_ws, od_bs,
           od_w_out, final_g):
    b, n_seq, d = x.shape
    n_ctx = ctx.shape[1]
    depth = ada_w.shape[0]
    heads = ev_a_log.shape[-1]
    qk_w = heads * HEAD_DIM
    gdn_qkv = 3 * qk_w
    gdn_in = gdn_qkv + 4 * heads
    pool_w_dim = ev_pool_scale.shape[-1]
    assert b + 1 <= MOD_ROWS and 4 * heads <= LANES and n_seq % GDN_CHUNK == 0 and n_ctx % GDN_CHUNK == 0
    assert qk_w % 1024 == 0 and pool_w_dim == qk_w

    cc = jnp.zeros((MOD_ROWS, d), F32).at[:b].set(c).at[b].set(c_ctx)
    mods = _ada_call(cc, ada_w, ada_b).reshape(depth, MOD_ROWS, N_ADA, 1, d)

    def batch_row(tm):
        return lambda i: (i * tm) // n_seq

    pos = _grid_pos_embed(n_seq, d)
    h = x.reshape(b * n_seq, d)
    last_even = 2 * ((depth - 1) // 2)
    assert last_even == 0, "context stream is only advanced through its first DeltaNet layer"

    for i in range(depth):
        j = i // 2
        fin = final_g if i == depth - 1 else None
        if i % 2 == 0:
            w_in = ev_w_in[j]
            ab_pad = LANES - 4 * heads
            n_cols = 5 * qk_w + LANES
            n_pad = -n_cols % 768
            w_x = jnp.concatenate([
                w_in[:, :gdn_qkv], w_in[:, gdn_in:gdn_in + 2 * qk_w], w_in[:, gdn_qkv:gdn_in],
                jnp.zeros((d, ab_pad + n_pad), F32)], axis=1).astype(BF16)
            w_c = jnp.concatenate([w_in[:, :gdn_in], jnp.zeros((d, ab_pad), F32)], axis=1).astype(BF16)
            blk = qk_w // LANES
            tm = _pick_tile(n_seq, (1024, 512, 256, 128))
            px, gate_x = _nm_call(h, pos, norm1_g[i], mods, i, 0, batch_row(tm), w_x, act="none", out_dtype=BF16,
                                  tm=tm, tn=768, rows_per_seq=n_seq, name="even_in", keep_col=5 * qk_w)
            tmc = _pick_tile(b * n_ctx, (1024, 512, 256, 128))
            pc, gate_c = _nm_call(ctx.reshape(b * n_ctx, d), None, norm1_g[i], mods, i, 0, lambda t: b, w_c,
                                  act="none", out_dtype=BF16, tm=tmc, tn=_pick_tile(w_c.shape[1], (640, 128)),
                                  rows_per_seq=b * n_ctx, name="even_in_ctx", keep_col=gdn_qkv)
            px = px.reshape(b, n_seq, -1)
            pc = pc.reshape(b, n_ctx, -1)

            lane_pad = jnp.zeros((LANES - 2 * heads,), F32)
            alog_row = jnp.concatenate([ev_a_log[j].reshape(-1), lane_pad]).reshape(1, LANES)
            dtb_row = jnp.concatenate([ev_dt_bias[j].reshape(-1), lane_pad]).reshape(1, LANES)
            gcol = jnp.concatenate([_gates_call(gate_c.reshape(b, n_ctx, LANES), 0, alog_row, dtb_row, heads),
                                    _gates_call(gate_x.reshape(b, n_seq, LANES), 0, alog_row, dtb_row, heads)],
                                   axis=1)
            y_a = _gdn_call(px, pc, gcol, _gate_rows(gcol, heads), ev_conv_w[j], ev_onorm_g[j],
                            heads, 0, blk, 2 * blk, 3 * blk)
            y_b = _pool_call(px, 4 * qk_w // pool_w_dim, ev_pool_w[j].astype(BF16), ev_pool_scale[j])
            y = jnp.concatenate([y_a, y_b], axis=-1).reshape(b * n_seq, -1)
            w_out = _cast_call(ev_w_out, j)
        else:
            tm = _pick_tile(n_seq, (1024, 512, 256, 128))
            w_od = _cast_call(od_w_in, j)
            zz = _nm_call(h, None, norm1_g[i], mods, i, 0, batch_row(tm), w_od, act="gelu", out_dtype=BF16,
                          tm=tm, tn=_pick_tile(w_od.shape[1], (2048, 1024, 512, 256, 128)), rows_per_seq=n_seq,
                          name="odd_in")
            y = _sgu_call(zz.reshape(b, n_seq, -1), od_ln_g[j], od_ln_b[j], od_ws[j].astype(BF16),
                          jnp.transpose(od_bs[j]), _pick_tile(n_seq, (512, 256, 128)))
            y = y.reshape(b * n_seq, -1)
            w_out = _cast_call(od_w_out, j)
        tm2 = _pick_tile(n_seq, (512, 256, 128))
        h = _mr_call(y, w_out, h, pos, mods, i, 2, None, tm=tm2,
                     tk=_pick_tile(y.shape[1], (2048, 1024, 512, 256, 128)), rows_per_seq=n_seq, name=f"mix_out{i}")
        pos = None
        h = _mlp(h, mods, i, norm2_g[i], _cast_call(mlp_w1, i), _cast_call(mlp_w2, i), fin, n_seq, batch_row)
    return h.reshape(b, n_seq, d)
```

```python
import functools
import math

import jax
import jax.numpy as jnp
import numpy as np
from jax import lax
from jax.experimental import pallas as pl
from jax.experimental.pallas import tpu as pltpu

F32 = jnp.float32
BF16 = jnp.bfloat16

EPS = 1e-6
GRID_W = 64
N_ADA = 6
HEAD_DIM = 128
GDN_CHUNK = 64
SHORT_CONV = 5
POOL_RADII = (1, 2, 4, 8)
SGU_GROUPS = 4
SGU_CHUNK = 128

LANES = 128
SUBLANES = 8
MOD_ROWS = 32
VMEM_LIMIT = 56 * 1024 * 1024

GDN_HEADS_PER_STEP = 2
GDN_CHUNK_GROUP = 9


def _params(*sem):
    return pltpu.CompilerParams(dimension_semantics=sem, vmem_limit_bytes=VMEM_LIMIT)


def _silu(x):
    return x * jax.nn.sigmoid(x)


def _cast_kernel(x_ref, o_ref):
    o_ref[...] = x_ref[...].astype(o_ref.dtype)


def _cast_call(w, layer):
    _, k, n = w.shape
    tr = _pick_tile(k, [t for t in (2048, 1024, 512, 256, 128, 64, 32, 16) if t * n * 4 <= 8 * 1024 * 1024])
    return pl.pallas_call(
        _cast_kernel,
        grid=(k // tr,),
        in_specs=[pl.BlockSpec((None, tr, n), lambda i: (layer, i, 0))],
        out_specs=pl.BlockSpec((tr, n), lambda i: (i, 0)),
        out_shape=jax.ShapeDtypeStruct((k, n), BF16),
        compiler_params=_params("arbitrary"),
        name="cast_weight",
    )(w)


def _ada_kernel(c_ref, w_ref, b_ref, o_ref):
    s = _silu(c_ref[...]).astype(BF16)
    o_ref[...] = jnp.dot(s, w_ref[...].astype(BF16), preferred_element_type=F32) + b_ref[...]


def _ada_call(cc, ada_w, ada_b):
    depth, d, n = ada_w.shape
    tn = 1024
    return pl.pallas_call(
        _ada_kernel,
        grid=(depth, n // tn),
        in_specs=[
            pl.BlockSpec((MOD_ROWS, d), lambda l, j: (0, 0)),
            pl.BlockSpec((None, d, tn), lambda l, j: (l, 0, j)),
            pl.BlockSpec((None, 1, tn), lambda l, j: (l, 0, j)),
        ],
        out_specs=pl.BlockSpec((None, MOD_ROWS, tn), lambda l, j: (l, 0, j)),
        out_shape=jax.ShapeDtypeStruct((depth, MOD_ROWS, n), F32),
        compiler_params=_params("arbitrary", "arbitrary"),
        name="ada",
    )(cc, ada_w, ada_b.reshape(depth, 1, n))


def _act(y, act):
    if act == "relu2":
        r = jnp.maximum(y, 0.0)
        return r * r
    if act == "gelu":
        return 0.5 * y * (1.0 + lax.erf(y * math.sqrt(0.5)))
    return y


def _nm_kernel(*refs, has_pos, act, row_chunk, keep):
    x_ref = refs[0]
    rest = list(refs[1:])
    pos_ref = rest.pop(0) if has_pos else None
    g_ref, sh_ref, sc_ref, w_ref, o_ref = rest[:5]
    keep_ref = rest[5] if keep is not None else None
    xn_ref = rest[-1]
    tm = x_ref.shape[0]

    @pl.when(pl.program_id(1) == 0)
    def _():
        gm = g_ref[...] * (1.0 + sc_ref[...])
        shift = sh_ref[...]

        def body(r, carry):
            rows = pl.ds(pl.multiple_of(r * row_chunk, row_chunk), row_chunk)
            x = x_ref[rows, :]
            if has_pos:
                x = x + pos_ref[rows, :]
            ms = jnp.mean(x * x, axis=-1, keepdims=True)
            xn_ref[rows, :] = (x * lax.rsqrt(ms + EPS) * gm + shift).astype(BF16)
            return carry

        lax.fori_loop(0, tm // row_chunk, body, 0, unroll=8)

    acc = jnp.dot(xn_ref[...], w_ref[...], preferred_element_type=F32)
    o_ref[...] = _act(acc, act).astype(o_ref.dtype)
    if keep is not None:
        keep_tile, keep_off = keep

        @pl.when(pl.program_id(1) == keep_tile)
        def _():
            keep_ref[...] = acc[:, keep_off:keep_off + LANES]


def _seq_major(i, tiles_per_seq, n_seqs):
    seq = i % n_seqs
    tile = i // n_seqs
    return seq * tiles_per_seq + tile, seq, tile


def _nm_call(x, pos, gain, mods, layer, k_shift, mod_row, w, *, act, out_dtype, tm, tn, rows_per_seq, name,
             keep_col=None):
    m, d = x.shape
    n = w.shape[1]
    has_pos = pos is not None
    tiles_per_seq = rows_per_seq // tm
    n_seqs = m // rows_per_seq

    def row_tile(i):
        return _seq_major(i, tiles_per_seq, n_seqs)[0] if has_pos else i

    in_specs = [pl.BlockSpec((tm, d), lambda i, j: (row_tile(i), 0))]
    args = [x]
    if has_pos:
        in_specs.append(pl.BlockSpec((tm, d), lambda i, j: (_seq_major(i, tiles_per_seq, n_seqs)[2], 0)))
        args.append(pos)
    in_specs += [
        pl.BlockSpec((1, d), lambda i, j: (0, 0)),
        pl.BlockSpec((None, None, None, 1, d), lambda i, j: (layer, mod_row(row_tile(i)), k_shift, 0, 0)),
        pl.BlockSpec((None, None, None, 1, d), lambda i, j: (layer, mod_row(row_tile(i)), k_shift + 1, 0, 0)),
        pl.BlockSpec((d, tn), lambda i, j: (0, j)),
    ]
    args += [gain.reshape(1, d), mods, mods, w]
    out_specs = [pl.BlockSpec((tm, tn), lambda i, j: (row_tile(i), j))]
    out_shape = [jax.ShapeDtypeStruct((m, n), out_dtype)]
    keep = None
    if keep_col is not None:
        keep = (keep_col // tn, keep_col % tn)
        assert keep[1] + LANES <= tn
        out_specs.append(pl.BlockSpec((tm, LANES), lambda i, j: (row_tile(i), 0)))
        out_shape.append(jax.ShapeDtypeStruct((m, LANES), F32))
    res = pl.pallas_call(
        functools.partial(_nm_kernel, has_pos=has_pos, act=act, row_chunk=16, keep=keep),
        grid=(m // tm, n // tn),
        in_specs=in_specs,
        out_specs=out_specs,
        out_shape=out_shape,
        scratch_shapes=[pltpu.VMEM((tm, d), BF16)],
        compiler_params=_params("arbitrary", "arbitrary"),
        name=name,
    )(*args)
    return res if keep_col is not None else res[0]


def _mr_kernel(*refs, has_pos, has_final, k_steps):
    a_ref, w_ref, h_ref, gate_ref = refs[:4]
    rest = list(refs[4:])
    pos_ref = rest.pop(0) if has_pos else None
    fg_ref = rest.pop(0) if has_final else None
    o_ref, *acc = rest

    def finish(total):
        h = h_ref[...]
        if has_pos:
            h = h + pos_ref[...]
        y = h + gate_ref[...] * total
        if has_final:
            ms = jnp.mean(y * y, axis=-1, keepdims=True)
            y = y * lax.rsqrt(ms + EPS) * fg_ref[...]
        o_ref[...] = y

    if k_steps == 1:
        finish(jnp.dot(a_ref[...], w_ref[...], preferred_element_type=F32))
        return
    acc_ref, = acc
    k = pl.program_id(1)

    @pl.when(k == 0)
    def _():
        acc_ref[...] = jnp.dot(a_ref[...], w_ref[...], preferred_element_type=F32)

    @pl.when(k > 0)
    def _():
        acc_ref[...] += jnp.dot(a_ref[...], w_ref[...], preferred_element_type=F32)

    @pl.when(k == k_steps - 1)
    def _():
        finish(acc_ref[...])


def _mr_call(a, w, h, pos, mods, layer, k_gate, final_g, *, tm, tk, rows_per_seq, name):
    m, kdim = a.shape
    d = w.shape[1]
    has_final = final_g is not None
    has_pos = pos is not None
    tiles_per_seq = rows_per_seq // tm
    n_seqs = m // rows_per_seq

    def row_tile(i):
        return _seq_major(i, tiles_per_seq, n_seqs)[0] if has_pos else i

    in_specs = [
        pl.BlockSpec((tm, tk), lambda i, k: (row_tile(i), k)),
        pl.BlockSpec((tk, d), lambda i, k: (k, 0)),
        pl.BlockSpec((tm, d), lambda i, k: (row_tile(i), 0)),
        pl.BlockSpec((None, None, None, 1, d), lambda i, k: (layer, row_tile(i) // tiles_per_seq, k_gate, 0, 0)),
    ]
    args = [a, w, h, mods]
    if has_pos:
        in_specs.append(pl.BlockSpec((tm, d), lambda i, k: (_seq_major(i, tiles_per_seq, n_seqs)[2], 0)))
        args.append(pos)
    if has_final:
        in_specs.append(pl.BlockSpec((1, d), lambda i, k: (0, 0)))
        args.append(final_g.reshape(1, d))
    k_steps = kdim // tk
    return pl.pallas_call(
        functools.partial(_mr_kernel, has_pos=has_pos, has_final=has_final, k_steps=k_steps),
        grid=(m // tm, k_steps),
        in_specs=in_specs,
        out_specs=pl.BlockSpec((tm, d), lambda i, k: (row_tile(i), 0)),
        out_shape=jax.ShapeDtypeStruct((m, d), F32),
        scratch_shapes=[pltpu.VMEM((tm, d), F32)] if k_steps > 1 else [],
        compiler_params=_params("arbitrary", "arbitrary"),
        name=name,
    )(*args)


def _gates_kernel(ab_ref, alog_ref, dtb_ref, o_ref, *, heads):
    x = ab_ref[...]
    n = x.shape[0]
    lane = lax.broadcasted_iota(jnp.int32, x.shape, 1)
    row = lax.broadcasted_iota(jnp.int32, x.shape, 0)
    pos = row % GDN_CHUNK
    g = -jnp.exp(alog_ref[...]) * jax.nn.softplus(x + dtb_ref[...])
    pre = g
    suf = g
    s = 1
    while s < GDN_CHUNK:
        pre = pre + jnp.where(pos >= s, pltpu.roll(pre, s, 0), 0.0)
        suf = suf + jnp.where(pos < GDN_CHUNK - s, pltpu.roll(suf, n - s, 0), 0.0)
        s *= 2
    beta = jax.nn.sigmoid(x)
    o_ref[...] = jnp.where(lane < heads, pre, jnp.where(lane < 2 * heads, suf, beta))


def _gates_call(p, col_block, alog_row, dtb_row, heads):
    b, n, _ = p.shape
    return pl.pallas_call(
        functools.partial(_gates_kernel, heads=heads),
        grid=(b,),
        in_specs=[
            pl.BlockSpec((None, n, LANES), lambda i: (i, 0, col_block)),
            pl.BlockSpec((1, LANES), lambda i: (0, 0)),
            pl.BlockSpec((1, LANES), lambda i: (0, 0)),
        ],
        out_specs=pl.BlockSpec((None, n, LANES), lambda i: (i, 0, 0)),
        out_shape=jax.ShapeDtypeStruct((b, n, LANES), F32),
        compiler_params=_params("arbitrary"),
        name="gdn_gates",
    )(p, alog_row, dtb_row)


def _l2norm(t):
    return t * lax.rsqrt(jnp.sum(t * t, axis=-1, keepdims=True) + EPS)


def _dot_nt(a, b):
    return lax.dot_general(a, b, (((1,), (1,)), ((), ())), preferred_element_type=F32)


def _dot_tn(a, b):
    return lax.dot_general(a, b, (((0,), (0,)), ((), ())), preferred_element_type=F32)


def _dot(a, b):
    return jnp.dot(a, b, preferred_element_type=F32)


def _tri_masks():
    c = GDN_CHUNK
    r = np.arange(c)[:, None]
    col = np.arange(2 * c)[None, :]
    j = col % c
    bwd = col >= c
    hi = np.where(bwd, j, r)
    lo = np.where(bwd, r, j)
    masks = [hi >= lo, hi > lo, r == j]
    s = 1
    while s < c:
        masks.append(((r ^ j) < 2 * s) & ((hi & s) != 0) & ((lo & s) == 0))
        s *= 2
    return np.stack(masks).astype(np.float32)


def _twice_diag(x16):
    z = jnp.zeros_like(x16)
    return jnp.concatenate([jnp.concatenate([x16, z], axis=1), jnp.concatenate([z, x16], axis=1)], axis=0)


def _gdn_kernel(qx_ref, kx_ref, vx_ref, zx_ref, qc_ref, kc_ref, vc_ref, gcol_ref, grow_ref, mask_ref, m16_ref,
                wq_ref, wk_ref, wv_ref, og_ref, y_ref,
                q_s, k_s, v_s, u_s, w_s, qd_s, kd_s, qkm_s, dl_s, o_s, st_s, pad_s,
                *, heads, hg, chunk_group, n_units):
    t_step = pl.program_id(0)
    groups = heads // hg
    hblk = jnp.minimum(t_step, n_units - 1) % groups
    slot_w = t_step % 2
    slot_s = 1 - slot_w
    n_x = qx_ref.shape[0]
    n_c = qc_ref.shape[0]
    c = GDN_CHUNK
    nc_c, nc_x = n_c // c, n_x // c
    n_chunks = nc_c + nc_x
    n_tot = n_c + n_x
    n_levels = mask_ref.shape[0] - 3
    halo = SUBLANES

    def stage1():
        pad_s[0:halo, :] = jnp.zeros((halo, LANES), F32)
        for hh in range(hg):
            lanes = slice(hh * LANES, (hh + 1) * LANES)
            for src_c, src_x, dst, w_ref, kind in ((qc_ref, qx_ref, q_s, wq_ref, "q"),
                                                   (kc_ref, kx_ref, k_s, wk_ref, "k"),
                                                   (vc_ref, vx_ref, v_s, wv_ref, "v")):
                w = w_ref[:, lanes]
                for src, base, n in ((src_c, 0, n_c), (src_x, n_c, n_x)):
                    rb = math.gcd(n, 256)

                    def copy_rows(i, carry, src=src, lanes=lanes, rb=rb):
                        r0 = pl.multiple_of(i * rb, rb)
                        pad_s[pl.ds(pl.multiple_of(halo + r0, halo), rb), :] = src[pl.ds(r0, rb), lanes].astype(F32)
                        return carry

                    def conv_rows(i, carry, base=base, w=w, kind=kind, dst=dst, hh=hh, rb=rb):
                        r0 = pl.multiple_of(i * rb, rb)
                        first = halo - (SHORT_CONV - 1) // 2 + r0
                        t = w[0:1, :] * pad_s[pl.ds(first, rb), :]
                        for tap in range(1, SHORT_CONV):
                            t = t + w[tap:tap + 1, :] * pad_s[pl.ds(first + tap, rb), :]
                        t = _silu(t)
                        if kind == "q":
                            t = _l2norm(t) * (HEAD_DIM ** -0.5)
                        elif kind == "k":
                            t = _l2norm(t)
                        dst[hh, pl.ds(pl.multiple_of(base + r0, halo), rb), :] = t
                        return carry

                    lax.fori_loop(0, n // rb, copy_rows, 0)
                    pad_s[halo + n:2 * halo + n, :] = jnp.zeros((halo, LANES), F32)
                    lax.fori_loop(0, n // rb, conv_rows, 0, unroll=min(4, n // rb))

    glane = lax.broadcasted_iota(jnp.int32, (c, LANES), 1)
    left = glane < c

    def block_diag(x):
        x16 = x.astype(BF16)
        return jnp.concatenate([x16 * m16_ref[0], x16 * m16_ref[1]], axis=0)

    def wy_group(it):
        items = []
        for hh in range(hg):
            for j in range(chunk_group):
                ch = it * chunk_group + j
                rows = pl.ds(pl.multiple_of(ch * c, c), c)
                head = hblk * hg + hh
                blk = gcol_ref[rows, :]

                def col(idx, blk=blk):
                    return jnp.broadcast_to(jnp.sum(jnp.where(glane == idx, blk, 0.0), axis=-1, keepdims=True), (c, LANES))

                k = k_s[hh, rows, :]
                k16 = k.astype(BF16)
                items.append(dict(hh=hh, ch=ch, rows=rows, k=k, k16=k16, q=q_s[hh, rows, :], v=v_s[hh, rows, :],
                                  gf=col(head), gb=col(heads + head), bf=col(2 * heads + head),
                                  bb=col(3 * heads + head), rp=grow_ref[hh, ch]))
        raws = [_dot_nt(jnp.concatenate([d["k16"], d["q"].astype(BF16)], axis=0),
                        jnp.concatenate([d["k16"], d["k16"]], axis=0)) for d in items]
        yield
        for d, raw in zip(items, raws):
            gam_col = jnp.where(left, d["gf"], d["gb"])
            beta_col = jnp.where(left, d["bf"], d["bb"])
            incl = mask_ref[0]
            diff = gam_col - d["rp"][0:1, :]
            decay = jnp.exp(jnp.where(incl > 0.0, diff, 0.0)) * incl
            a = raw[:c] * decay * beta_col * mask_ref[1]
            d["a16"] = a.astype(BF16)
            qkm_s[slot_w, d["hh"], d["ch"]] = (raw[c:] * decay).astype(BF16)
            d["t"] = mask_ref[2] - a * mask_ref[3]
        for lvl in range(1, n_levels):
            xs = [_dot(d["a16"] * m16_ref[2 + lvl], block_diag(d["t"])) for d in items]
            yield
            ys = [_dot(d["t"].astype(BF16), block_diag(x)) for d, x in zip(items, xs)]
            yield
            for d, y in zip(items, ys):
                d["t"] = d["t"] - y
        us, ws = [], []
        for d in items:
            beta_row = d["rp"][1:2, :]
            scale_w = beta_row * jnp.exp(d["rp"][0:1, :])
            us.append(_dot((d["t"] * beta_row).astype(BF16), _twice_diag(d["v"].astype(BF16))))
            ws.append(_dot((d["t"] * scale_w).astype(BF16), _twice_diag(d["k16"])))
        yield
        for d, u, w in zip(items, us, ws):
            hh, ch, rows = d["hh"], d["ch"], d["rows"]
            for dirn, gam in ((0, d["gf"]), (1, d["gb"])):
                lanes = slice(dirn * LANES, (dirn + 1) * LANES)
                g_last = gam[c - 1:c, :] if dirn == 0 else gam[0:1, :]
                u_s[slot_w, hh, dirn, rows, :] = u[:, lanes].astype(BF16)
                w_s[slot_w, hh, dirn, rows, :] = w[:, lanes].astype(BF16)
                qd_s[slot_w, hh, dirn, rows, :] = (d["q"] * jnp.exp(gam)).astype(BF16)
                kd_s[slot_w, hh, dirn, rows, :] = (d["k"] * jnp.exp(g_last - gam)).astype(BF16)
                dl_s[slot_w, hh, dirn, ch] = jnp.broadcast_to(jnp.exp(g_last), (SUBLANES, LANES))

    chains = [(hh, dirn) for hh in range(hg) for dirn in range(2)]

    def scan_step(s, states):
        ch_b = jnp.where(s < nc_c, nc_c - 1 - s, n_chunks + nc_c - 1 - s)
        chunk = [s if dirn == 0 else ch_b for hh, dirn in chains]
        rows = [pl.ds(pl.multiple_of(ch * c, c), c) for ch in chunk]
        s16 = [st.astype(BF16) for st in states]
        r1 = [_dot(jnp.concatenate([w_s[slot_s, hh, dirn, rw, :], qd_s[slot_s, hh, dirn, rw, :]], axis=0), st)
              for (hh, dirn), rw, st in zip(chains, rows, s16)]
        yield
        vn16 = [(u_s[slot_s, hh, dirn, rw, :].astype(F32) - r[:c]).astype(BF16)
                for (hh, dirn), rw, r in zip(chains, rows, r1)]
        upd = [_dot_tn(kd_s[slot_s, hh, dirn, rw, :], vn) for (hh, dirn), rw, vn in zip(chains, rows, vn16)]
        zero = jnp.zeros((c, LANES), BF16)
        intra = [_dot(qkm_s[slot_s, hh, ch], jnp.concatenate([vn, zero] if dirn == 0 else [zero, vn], axis=0))
                 for (hh, dirn), ch, vn in zip(chains, chunk, vn16)]
        yield
        for (hh, dirn), rw, r, o in zip(chains, rows, r1, intra):
            o_s[hh, rw, :] += r[c:] + o
        states[:] = [st * dl_s[slot_s, hh, dirn, ch][0:1, :] + up
                     for (hh, dirn), ch, st, up in zip(chains, chunk, states, upd)]

    def scan_group(it):
        states = [st_s[i] for i in range(len(chains))]
        for j in range(chunk_group):
            yield from scan_step(it * chunk_group + j, states)
        for i, st in enumerate(states):
            st_s[i] = st

    def interleave(*gens):
        live = [[gen, 0, n_stages] for gen, n_stages in gens]
        while live:
            live.sort(key=lambda e: e[1] / e[2])
            entry = live[0]
            try:
                next(entry[0])
                entry[1] += 1
            except StopIteration:
                live.remove(entry)

    def scan_init():
        for i in range(len(chains)):
            st_s[i] = jnp.zeros((HEAD_DIM, HEAD_DIM), F32)
        zrows = math.gcd(n_tot, 256)

        def zero_rows(i, carry):
            for hh in range(hg):
                o_s[hh, pl.ds(pl.multiple_of(i * zrows, zrows), zrows), :] = jnp.zeros((zrows, LANES), F32)
            return carry

        lax.fori_loop(0, n_tot // zrows, zero_rows, 0)

    def stage4():
        rb = math.gcd(n_x, 256)
        for hh in range(hg):
            lanes = slice(hh * LANES, (hh + 1) * LANES)

            def gate_rows(i, carry, hh=hh, lanes=lanes):
                rows = pl.ds(pl.multiple_of(i * rb, rb), rb)
                o = o_s[hh, pl.ds(pl.multiple_of(n_c + i * rb, math.gcd(n_c, rb)), rb), :]
                ms = jnp.mean(o * o, axis=-1, keepdims=True)
                y = o * lax.rsqrt(ms + EPS) * og_ref[...]
                y_ref[rows, lanes] = (y * _silu(zx_ref[rows, lanes].astype(F32))).astype(y_ref.dtype)
                return carry

            lax.fori_loop(0, n_x // rb, gate_rows, 0, unroll=min(4, n_x // rb))

    def run(do_wy, do_scan):
        if do_wy:
            stage1()
        if do_scan:
            scan_init()

        def group(it, carry):
            gens = []
            if do_wy:
                gens.append((wy_group(it), 2 * n_levels + 1))
            if do_scan:
                gens.append((scan_group(it), 2 * chunk_group + 1))
            interleave(*gens)
            return carry

        lax.fori_loop(0, n_chunks // chunk_group, group, 0)
        if do_scan:
            stage4()

    @pl.when(t_step == 0)
    def _():
        run(True, False)

    @pl.when(jnp.logical_and(t_step > 0, t_step < n_units))
    def _():
        run(True, True)

    @pl.when(t_step == n_units)
    def _():
        run(False, True)


def _gdn_call(px, pc, gcol, grow, conv_w, onorm_g, heads, blk_q, blk_k, blk_v, blk_z):
    b, n_x, _ = px.shape
    n_c = pc.shape[1]
    c = GDN_CHUNK
    n_tot = n_c + n_x
    n_chunks = n_tot // c
    hg = GDN_HEADS_PER_STEP if heads % GDN_HEADS_PER_STEP == 0 else 1
    chunk_group = max(g for g in range(1, GDN_CHUNK_GROUP + 1) if n_chunks % g == 0)
    assert all(blk % hg == 0 for blk in (blk_q, blk_k, blk_v, blk_z))
    wide = hg * LANES
    groups = heads // hg
    n_units = b * groups
    tri = _tri_masks()
    masks = jnp.asarray(tri)
    lane_half = np.arange(2 * c)[None, :] < c
    half_masks = np.stack([np.broadcast_to(lane_half, (c, 2 * c)), np.broadcast_to(~lane_half, (c, 2 * c))])
    masks16 = jnp.asarray(np.concatenate([half_masks.astype(np.float32), tri[3:]]), dtype=BF16)

    def build(t):
        u = jnp.minimum(t, n_units - 1)
        return u // groups, u % groups

    def finish(t):
        u = jnp.maximum(t - 1, 0)
        return u // groups, u % groups

    def seq_spec(n, blk0, which):
        return pl.BlockSpec((None, n, wide), lambda t: (which(t)[0], 0, blk0 // hg + which(t)[1]))

    def wspec(blk0):
        return pl.BlockSpec((SHORT_CONV, wide), lambda t: (0, blk0 // hg + build(t)[1]))

    in_specs = [
        seq_spec(n_x, blk_q, build), seq_spec(n_x, blk_k, build), seq_spec(n_x, blk_v, build),
        seq_spec(n_x, blk_z, finish),
        seq_spec(n_c, blk_q, build), seq_spec(n_c, blk_k, build), seq_spec(n_c, blk_v, build),
        pl.BlockSpec((None, n_tot, LANES), lambda t: (build(t)[0], 0, 0)),
        pl.BlockSpec((None, hg, n_chunks, SUBLANES, LANES), lambda t: (build(t)[0], build(t)[1], 0, 0, 0)),
        pl.BlockSpec(masks.shape, lambda t: (0, 0, 0)),
        pl.BlockSpec(masks16.shape, lambda t: (0, 0, 0)),
        wspec(0), wspec(heads), wspec(2 * heads),
        pl.BlockSpec((1, LANES), lambda t: (0, 0)),
    ]
    per_dir = pltpu.VMEM((2, hg, 2, n_tot, LANES), BF16)
    scratch = [pltpu.VMEM((hg, n_tot, LANES), F32)] * 3 + [
        per_dir, per_dir, per_dir, per_dir,
        pltpu.VMEM((2, hg, n_chunks, c, 2 * c), BF16),
        pltpu.VMEM((2, hg, 2, n_chunks, SUBLANES, LANES), F32),
        pltpu.VMEM((hg, n_tot, LANES), F32),
        pltpu.VMEM((2 * hg, HEAD_DIM, HEAD_DIM), F32),
        pltpu.VMEM((max(n_x, n_c) + 2 * SUBLANES, LANES), F32),
    ]
    return pl.pallas_call(
        functools.partial(_gdn_kernel, heads=heads, hg=hg, chunk_group=chunk_group, n_units=n_units),
        grid=(n_units + 1,),
        in_specs=in_specs,
        out_specs=pl.BlockSpec((None, n_x, wide), lambda t: (finish(t)[0], 0, finish(t)[1])),
        out_shape=jax.ShapeDtypeStruct((b, n_x, heads * HEAD_DIM), BF16),
        scratch_shapes=scratch,
        compiler_params=_params("arbitrary"),
        name="gdn",
    )(px, px, px, px, pc, pc, pc, gcol, grow, masks, masks16, conv_w, conv_w, conv_w, onorm_g.reshape(1, LANES))


def _pool_kernel(p_ref, w_ref, s_ref, o_ref, pad_s):
    n = p_ref.shape[0]
    gc = w_ref.shape[1]
    slabs = pad_s.shape[0]
    halo = pad_s.shape[1] - n
    half = halo // 2
    for sl in range(slabs):
        pad_s[sl, 0:half, :] = jnp.zeros((half, LANES), F32)
        pad_s[sl, half + n:halo + n, :] = jnp.zeros((half, LANES), F32)
    rb = math.gcd(n, 256)
    for gi, r in enumerate(POOL_RADII):
        cols = slice(gi * gc, (gi + 1) * gc)

        def copy_rows(i, carry):
            r0 = pl.multiple_of(i * rb, rb)
            for sl in range(slabs):
                lanes = slice(gi * gc + sl * LANES, gi * gc + (sl + 1) * LANES)
                pad_s[sl, pl.ds(pl.multiple_of(half + r0, SUBLANES), rb), :] = p_ref[pl.ds(r0, rb), lanes].astype(F32)
            return carry

        def pool_rows(i, carry):
            r0 = pl.multiple_of(i * rb, rb)
            tr = lax.broadcasted_iota(jnp.int32, (rb, 1), 0) + r0
            inv_cnt = 1.0 / (jnp.minimum(tr + r + 1, n) - jnp.maximum(tr - r, 0)).astype(F32)
            parts = []
            for sl in range(slabs):
                tot = pad_s[sl, pl.ds(half - r + r0, rb), :]
                for s in range(1 - r, r + 1):
                    tot = tot + pad_s[sl, pl.ds(half + s + r0, rb), :]
                centre = pad_s[sl, pl.ds(pl.multiple_of(half + r0, SUBLANES), rb), :]
                parts.append((tot * inv_cnt - centre).astype(BF16))
            y = jnp.dot(jnp.concatenate(parts, axis=1), w_ref[gi], preferred_element_type=F32)
            o_ref[pl.ds(r0, rb), cols] = (y * s_ref[:, cols]).astype(o_ref.dtype)
            return carry

        lax.fori_loop(0, n // rb, copy_rows, 0)
        lax.fori_loop(0, n // rb, pool_rows, 0, unroll=min(2, n // rb))


def _pool_call(px, col_block, pool_w, pool_scale):
    b, n, _ = px.shape
    g, gc, _ = pool_w.shape
    width = g * gc
    return pl.pallas_call(
        _pool_kernel,
        grid=(b,),
        in_specs=[
            pl.BlockSpec((None, n, width), lambda i: (i, 0, col_block)),
            pl.BlockSpec((g, gc, gc), lambda i: (0, 0, 0)),
            pl.BlockSpec((1, width), lambda i: (0, 0)),
        ],
        out_specs=pl.BlockSpec((None, n, width), lambda i: (i, 0, 0)),
        out_shape=jax.ShapeDtypeStruct((b, n, width), BF16),
        scratch_shapes=[pltpu.VMEM((gc // LANES, n + 2 * SUBLANES * pl.cdiv(max(POOL_RADII), SUBLANES), LANES), F32)],
        compiler_params=_params("arbitrary"),
        name="pool",
    )(px, pool_w, pool_scale.reshape(1, width))


def _sgu_kernel(u_ref, v_ref, lg_ref, lb_ref, ws_ref, bs_ref, o_ref, vn_ref):
    tr, wdt = v_ref.shape
    gw = wdt // SGU_GROUPS
    rc = 64

    def ln_body(r, carry):
        rows = pl.ds(pl.multiple_of(r * rc, rc), rc)
        v = v_ref[rows, :].astype(F32)
        mu = jnp.mean(v, axis=-1, keepdims=True)
        d = v - mu
        var = jnp.mean(d * d, axis=-1, keepdims=True)
        vn_ref[rows, :] = (d * lax.rsqrt(var + EPS) * lg_ref[...] + lb_ref[...]).astype(BF16)
        return carry

    lax.fori_loop(0, tr // rc, ln_body, 0, unroll=2)
    for ch in range(tr // SGU_CHUNK):
        rows = slice(ch * SGU_CHUNK, (ch + 1) * SGU_CHUNK)
        for g in range(SGU_GROUPS):
            cols = slice(g * gw, (g + 1) * gw)
            mixed = jnp.dot(ws_ref[g], vn_ref[rows, cols], preferred_element_type=F32) + bs_ref[:, g:g + 1]
            o_ref[rows, cols] = (u_ref[rows, cols].astype(F32) * mixed).astype(o_ref.dtype)


def _sgu_call(zz, ln_g, ln_b, ws, bs_t, tr):
    b, n, w2 = zz.shape
    wdt = w2 // 2
    return pl.pallas_call(
        _sgu_kernel,
        grid=(b, n // tr),
        in_specs=[
            pl.BlockSpec((None, tr, wdt), lambda i, r: (i, r, 0)),
            pl.BlockSpec((None, tr, wdt), lambda i, r: (i, r, 1)),
            pl.BlockSpec((1, wdt), lambda i, r: (0, 0)),
            pl.BlockSpec((1, wdt), lambda i, r: (0, 0)),
            pl.BlockSpec((SGU_GROUPS, SGU_CHUNK, SGU_CHUNK), lambda i, r: (0, 0, 0)),
            pl.BlockSpec((SGU_CHUNK, SGU_GROUPS), lambda i, r: (0, 0)),
        ],
        out_specs=pl.BlockSpec((None, tr, wdt), lambda i, r: (i, r, 0)),
        out_shape=jax.ShapeDtypeStruct((b, n, wdt), BF16),
        scratch_shapes=[pltpu.VMEM((tr, wdt), BF16)],
        compiler_params=_params("arbitrary", "arbitrary"),
        name="sgu",
    )(zz, zz, ln_g.reshape(1, wdt), ln_b.reshape(1, wdt), ws, bs_t)


def _grid_pos_embed(n, d):
    rows = n // GRID_W
    nf = d // 4
    omega = 1.0 / (10000.0 ** (jnp.arange(nf, dtype=F32) / nf))
    r = jnp.repeat(jnp.arange(rows, dtype=F32), GRID_W)
    col = jnp.tile(jnp.arange(GRID_W, dtype=F32), rows)
    ar = r[:, None] * omega
    ac = col[:, None] * omega
    return jnp.concatenate([jnp.sin(ar), jnp.cos(ar), jnp.sin(ac), jnp.cos(ac)], axis=-1)


def _pick_tile(n, candidates):
    for t in candidates:
        if n % t == 0:
            return t
    raise ValueError(f"no tile for {n}")


def _gate_rows(gcol, heads):
    b, n, _ = gcol.shape
    c = GDN_CHUNK
    g = gcol[:, :, :4 * heads].reshape(b, n // c, c, 2, 2, heads)
    g = jnp.transpose(g, (0, 5, 1, 3, 4, 2)).reshape(b, heads, n // c, 2, 2 * c)
    return jnp.pad(g, ((0, 0), (0, 0), (0, 0), (0, SUBLANES - 2), (0, 0)))


def _mlp(h, mods, layer, norm_g, w1, w2, final_g, n_seq, batch_row):
    tm = _pick_tile(n_seq, (1024, 512, 256, 128))
    hid = _nm_call(h, None, norm_g, mods, layer, 3, batch_row(tm), w1, act="relu2", out_dtype=BF16,
                   tm=tm, tn=_pick_tile(w1.shape[1], (2048, 1024, 512, 256, 128)), rows_per_seq=n_seq,
                   name=f"mlp_up{layer}")
    tm2 = _pick_tile(n_seq, (512, 256, 128))
    return _mr_call(hid, w2, h, None, mods, layer, 5, final_g, tm=tm2,
                    tk=_pick_tile(w2.shape[0], (2048, 1024, 512, 256, 128)), rows_per_seq=n_seq,
                    name=f"mlp_down{layer}")


def kernel(x, c, ctx, c_ctx, ada_w, ada_b, norm1_g, norm2_g, mlp_w1, mlp_w2, ev_w_in, ev_conv_w, ev_a_log,
           ev_dt_bias, ev_onorm_g, ev_pool_w, ev_pool_scale, ev_w_out, od_w_in, od_ln_g, od_ln_b, od_ws, od_bs,
           od_w_out, final_g):
    b, n_seq, d = x.shape
    n_ctx = ctx.shape[1]
    depth = ada_w.shape[0]
    heads = ev_a_log.shape[-1]
    qk_w = heads * HEAD_DIM
    gdn_qkv = 3 * qk_w
    gdn_in = gdn_qkv + 4 * heads
    pool_w_dim = ev_pool_scale.shape[-1]
    assert b + 1 <= MOD_ROWS and 4 * heads <= LANES and n_seq % GDN_CHUNK == 0 and n_ctx % GDN_CHUNK == 0
    assert qk_w % 1024 == 0 and pool_w_dim == qk_w

    cc = jnp.zeros((MOD_ROWS, d), F32).at[:b].set(c).at[b].set(c_ctx)
    mods = _ada_call(cc, ada_w, ada_b).reshape(depth, MOD_ROWS, N_ADA, 1, d)

    def batch_row(tm):
        return lambda i: (i * tm) // n_seq

    pos = _grid_pos_embed(n_seq, d)
    h = x.reshape(b * n_seq, d)
    last_even = 2 * ((depth - 1) // 2)
    assert last_even == 0, "context stream is only advanced through its first DeltaNet layer"

    for i in range(depth):
        j = i // 2
        fin = final_g if i == depth - 1 else None
        if i % 2 == 0:
            w_in = _cast_call(ev_w_in, j)
            ab_pad = LANES - 4 * heads
            n_cols = 5 * qk_w + LANES
            n_pad = -n_cols % 768
            w_x = jnp.concatenate([
                w_in[:, :gdn_qkv], w_in[:, gdn_in:gdn_in + 2 * qk_w], w_in[:, gdn_qkv:gdn_in],
                jnp.zeros((d, ab_pad + n_pad), BF16)], axis=1)
            w_c = jnp.concatenate([w_in[:, :gdn_in], jnp.zeros((d, ab_pad), BF16)], axis=1)
            blk = qk_w // LANES
            tm = _pick_tile(n_seq, (1024, 512, 256, 128))
            px, gate_x = _nm_call(h, pos, norm1_g[i], mods, i, 0, batch_row(tm), w_x, act="none", out_dtype=BF16,
                                  tm=tm, tn=768, rows_per_seq=n_seq, name="even_in", keep_col=5 * qk_w)
            tmc = _pick_tile(b * n_ctx, (1024, 512, 256, 128))
            pc, gate_c = _nm_call(ctx.reshape(b * n_ctx, d), None, norm1_g[i], mods, i, 0, lambda t: b, w_c,
                                  act="none", out_dtype=BF16, tm=tmc, tn=_pick_tile(w_c.shape[1], (640, 128)),
                                  rows_per_seq=b * n_ctx, name="even_in_ctx", keep_col=gdn_qkv)
            px = px.reshape(b, n_seq, -1)
            pc = pc.reshape(b, n_ctx, -1)

            lane_pad = jnp.zeros((LANES - 2 * heads,), F32)
            alog_row = jnp.concatenate([ev_a_log[j].reshape(-1), lane_pad]).reshape(1, LANES)
            dtb_row = jnp.concatenate([ev_dt_bias[j].reshape(-1), lane_pad]).reshape(1, LANES)
            gcol = jnp.concatenate([_gates_call(gate_c.reshape(b, n_ctx, LANES), 0, alog_row, dtb_row, heads),
                                    _gates_call(gate_x.reshape(b, n_seq, LANES), 0, alog_row, dtb_row, heads)],
                                   axis=1)
            y_a = _gdn_call(px, pc, gcol, _gate_rows(gcol, heads), ev_conv_w[j], ev_onorm_g[j],
                            heads, 0, blk, 2 * blk, 3 * blk)
            y_b = _pool_call(px, 4 * qk_w // pool_w_dim, ev_pool_w[j].astype(BF16), ev_pool_scale[j])
            y = jnp.concatenate([y_a, y_b], axis=-1).reshape(b * n_seq, -1)
            w_out = _cast_call(ev_w_out, j)
        else:
            tm = _pick_tile(n_seq, (1024, 512, 256, 128))
            w_od = _cast_call(od_w_in, j)
            zz = _nm_call(h, None, norm1_g[i], mods, i, 0, batch_row(tm), w_od, act="gelu", out_dtype=BF16,
                          tm=tm, tn=_pick_tile(w_od.shape[1], (2048, 1024, 512, 256, 128)), rows_per_seq=n_seq,
                          name="odd_in")
            y = _sgu_call(zz.reshape(b, n_seq, -1), od_ln_g[j], od_ln_b[j], od_ws[j].astype(BF16),
                          jnp.transpose(od_bs[j]), _pick_tile(n_seq, (512, 256, 128)))
            y = y.reshape(b * n_seq, -1)
            w_out = _cast_call(od_w_out, j)
        tm2 = _pick_tile(n_seq, (512, 256, 128))
        h = _mr_call(y, w_out, h, pos, mods, i, 2, None, tm=tm2,
                     tk=_pick_tile(y.shape[1], (2048, 1024, 512, 256, 128)), rows_per_seq=n_seq, name=f"mix_out{i}")
        pos = None
        h = _mlp(h, mods, i, norm2_g[i], _cast_call(mlp_w1, i), _cast_call(mlp_w2, i), fin, n_seq, batch_row)
    return h.reshape(b, n_seq, d)
```

```python
import functools
import math

import jax
import jax.numpy as jnp
import numpy as np
from jax import lax
from jax.experimental import pallas as pl
from jax.experimental.pallas import tpu as pltpu

F32 = jnp.float32
BF16 = jnp.bfloat16

EPS = 1e-6
GRID_W = 64
N_ADA = 6
HEAD_DIM = 128
GDN_CHUNK = 64
SHORT_CONV = 5
POOL_RADII = (1, 2, 4, 8)
SGU_GROUPS = 4
SGU_CHUNK = 128

LANES = 128
SUBLANES = 8
MOD_ROWS = 32
VMEM_LIMIT = 56 * 1024 * 1024

GDN_HEADS_PER_STEP = 2
GDN_CHUNK_GROUP = 9


def _params(*sem):
    return pltpu.CompilerParams(dimension_semantics=sem, vmem_limit_bytes=VMEM_LIMIT)


def _silu(x):
    return x * jax.nn.sigmoid(x)


def _cast_kernel(x_ref, o_ref):
    o_ref[...] = x_ref[...].astype(o_ref.dtype)


def _cast_call(w, layer):
    _, k, n = w.shape
    tr = _pick_tile(k, [t for t in (2048, 1024, 512, 256, 128, 64, 32, 16) if t * n * 4 <= 8 * 1024 * 1024])
    return pl.pallas_call(
        _cast_kernel,
        grid=(k // tr,),
        in_specs=[pl.BlockSpec((None, tr, n), lambda i: (layer, i, 0))],
        out_specs=pl.BlockSpec((tr, n), lambda i: (i, 0)),
        out_shape=jax.ShapeDtypeStruct((k, n), BF16),
        compiler_params=_params("arbitrary"),
        name="cast_weight",
    )(w)


def _ada_kernel(c_ref, w_ref, b_ref, o_ref):
    s = _silu(c_ref[...]).astype(BF16)
    o_ref[...] = jnp.dot(s, w_ref[...].astype(BF16), preferred_element_type=F32) + b_ref[...]


def _ada_call(cc, ada_w, ada_b):
    depth, d, n = ada_w.shape
    tn = 1024
    return pl.pallas_call(
        _ada_kernel,
        grid=(depth, n // tn),
        in_specs=[
            pl.BlockSpec((MOD_ROWS, d), lambda l, j: (0, 0)),
            pl.BlockSpec((None, d, tn), lambda l, j: (l, 0, j)),
            pl.BlockSpec((None, 1, tn), lambda l, j: (l, 0, j)),
        ],
        out_specs=pl.BlockSpec((None, MOD_ROWS, tn), lambda l, j: (l, 0, j)),
        out_shape=jax.ShapeDtypeStruct((depth, MOD_ROWS, n), F32),
        compiler_params=_params("arbitrary", "arbitrary"),
        name="ada",
    )(cc, ada_w, ada_b.reshape(depth, 1, n))


def _act(y, act):
    if act == "relu2":
        r = jnp.maximum(y, 0.0)
        return r * r
    if act == "gelu":
        return 0.5 * y * (1.0 + lax.erf(y * math.sqrt(0.5)))
    return y


def _nm_kernel(*refs, has_pos, act, row_chunk, keep):
    x_ref = refs[0]
    rest = list(refs[1:])
    pos_ref = rest.pop(0) if has_pos else None
    g_ref, sh_ref, sc_ref, w_ref, o_ref = rest[:5]
    keep_ref = rest[5] if keep is not None else None
    xn_ref = rest[-1]
    tm = x_ref.shape[0]

    @pl.when(pl.program_id(1) == 0)
    def _():
        gm = g_ref[...] * (1.0 + sc_ref[...])
        shift = sh_ref[...]

        def body(r, carry):
            rows = pl.ds(pl.multiple_of(r * row_chunk, row_chunk), row_chunk)
            x = x_ref[rows, :]
            if has_pos:
                x = x + pos_ref[rows, :]
            ms = jnp.mean(x * x, axis=-1, keepdims=True)
            xn_ref[rows, :] = (x * lax.rsqrt(ms + EPS) * gm + shift).astype(BF16)
            return carry

        lax.fori_loop(0, tm // row_chunk, body, 0, unroll=8)

    acc = jnp.dot(xn_ref[...], w_ref[...], preferred_element_type=F32)
    o_ref[...] = _act(acc, act).astype(o_ref.dtype)
    if keep is not None:
        keep_tile, keep_off = keep

        @pl.when(pl.program_id(1) == keep_tile)
        def _():
            keep_ref[...] = acc[:, keep_off:keep_off + LANES]


def _seq_major(i, tiles_per_seq, n_seqs):
    seq = i % n_seqs
    tile = i // n_seqs
    return seq * tiles_per_seq + tile, seq, tile


def _nm_call(x, pos, gain, mods, layer, k_shift, mod_row, w, *, act, out_dtype, tm, tn, rows_per_seq, name,
             keep_col=None):
    m, d = x.shape
    n = w.shape[1]
    has_pos = pos is not None
    tiles_per_seq = rows_per_seq // tm
    n_seqs = m // rows_per_seq

    def row_tile(i):
        return _seq_major(i, tiles_per_seq, n_seqs)[0] if has_pos else i

    in_specs = [pl.BlockSpec((tm, d), lambda i, j: (row_tile(i), 0))]
    args = [x]
    if has_pos:
        in_specs.append(pl.BlockSpec((tm, d), lambda i, j: (_seq_major(i, tiles_per_seq, n_seqs)[2], 0)))
        args.append(pos)
    in_specs += [
        pl.BlockSpec((1, d), lambda i, j: (0, 0)),
        pl.BlockSpec((None, None, None, 1, d), lambda i, j: (layer, mod_row(row_tile(i)), k_shift, 0, 0)),
        pl.BlockSpec((None, None, None, 1, d), lambda i, j: (layer, mod_row(row_tile(i)), k_shift + 1, 0, 0)),
        pl.BlockSpec((d, tn), lambda i, j: (0, j)),
    ]
    args += [gain.reshape(1, d), mods, mods, w]
    out_specs = [pl.BlockSpec((tm, tn), lambda i, j: (row_tile(i), j))]
    out_shape = [jax.ShapeDtypeStruct((m, n), out_dtype)]
    keep = None
    if keep_col is not None:
        keep = (keep_col // tn, keep_col % tn)
        assert keep[1] + LANES <= tn
        out_specs.append(pl.BlockSpec((tm, LANES), lambda i, j: (row_tile(i), 0)))
        out_shape.append(jax.ShapeDtypeStruct((m, LANES), F32))
    res = pl.pallas_call(
        functools.partial(_nm_kernel, has_pos=has_pos, act=act, row_chunk=16, keep=keep),
        grid=(m // tm, n // tn),
        in_specs=in_specs,
        out_specs=out_specs,
        out_shape=out_shape,
        scratch_shapes=[pltpu.VMEM((tm, d), BF16)],
        compiler_params=_params("arbitrary", "arbitrary"),
        name=name,
    )(*args)
    return res if keep_col is not None else res[0]


def _mr_kernel(*refs, has_pos, has_final, k_steps):
    a_ref, w_ref, h_ref, gate_ref = refs[:4]
    rest = list(refs[4:])
    pos_ref = rest.pop(0) if has_pos else None
    fg_ref = rest.pop(0) if has_final else None
    o_ref, *acc = rest

    def finish(total):
        h = h_ref[...]
        if has_pos:
            h = h + pos_ref[...]
        y = h + gate_ref[...] * total
        if has_final:
            ms = jnp.mean(y * y, axis=-1, keepdims=True)
            y = y * lax.rsqrt(ms + EPS) * fg_ref[...]
        o_ref[...] = y

    if k_steps == 1:
        finish(jnp.dot(a_ref[...], w_ref[...], preferred_element_type=F32))
        return
    acc_ref, = acc
    k = pl.program_id(1)

    @pl.when(k == 0)
    def _():
        acc_ref[...] = jnp.dot(a_ref[...], w_ref[...], preferred_element_type=F32)

    @pl.when(k > 0)
    def _():
        acc_ref[...] += jnp.dot(a_ref[...], w_ref[...], preferred_element_type=F32)

    @pl.when(k == k_steps - 1)
    def _():
        finish(acc_ref[...])


def _mr_call(a, w, h, pos, mods, layer, k_gate, final_g, *, tm, tk, rows_per_seq, name):
    m, kdim = a.shape
    d = w.shape[1]
    has_final = final_g is not None
    has_pos = pos is not None
    tiles_per_seq = rows_per_seq // tm
    n_seqs = m // rows_per_seq

    def row_tile(i):
        return _seq_major(i, tiles_per_seq, n_seqs)[0] if has_pos else i

    in_specs = [
        pl.BlockSpec((tm, tk), lambda i, k: (row_tile(i), k)),
        pl.BlockSpec((tk, d), lambda i, k: (k, 0)),
        pl.BlockSpec((tm, d), lambda i, k: (row_tile(i), 0)),
        pl.BlockSpec((None, None, None, 1, d), lambda i, k: (layer, row_tile(i) // tiles_per_seq, k_gate, 0, 0)),
    ]
    args = [a, w, h, mods]
    if has_pos:
        in_specs.append(pl.BlockSpec((tm, d), lambda i, k: (_seq_major(i, tiles_per_seq, n_seqs)[2], 0)))
        args.append(pos)
    if has_final:
        in_specs.append(pl.BlockSpec((1, d), lambda i, k: (0, 0)))
        args.append(final_g.reshape(1, d))
    k_steps = kdim // tk
    return pl.pallas_call(
        functools.partial(_mr_kernel, has_pos=has_pos, has_final=has_final, k_steps=k_steps),
        grid=(m // tm, k_steps),
        in_specs=in_specs,
        out_specs=pl.BlockSpec((tm, d), lambda i, k: (row_tile(i), 0)),
        out_shape=jax.ShapeDtypeStruct((m, d), F32),
        scratch_shapes=[pltpu.VMEM((tm, d), F32)] if k_steps > 1 else [],
        compiler_params=_params("arbitrary", "arbitrary"),
        name=name,
    )(*args)


def _gates_kernel(ab_ref, alog_ref, dtb_ref, o_ref, *, heads):
    x = ab_ref[...]
    n = x.shape[0]
    lane = lax.broadcasted_iota(jnp.int32, x.shape, 1)
    row = lax.broadcasted_iota(jnp.int32, x.shape, 0)
    pos = row % GDN_CHUNK
    g = -jnp.exp(alog_ref[...]) * jax.nn.softplus(x + dtb_ref[...])
    pre = g
    suf = g
    s = 1
    while s < GDN_CHUNK:
        pre = pre + jnp.where(pos >= s, pltpu.roll(pre, s, 0), 0.0)
        suf = suf + jnp.where(pos < GDN_CHUNK - s, pltpu.roll(suf, n - s, 0), 0.0)
        s *= 2
    beta = jax.nn.sigmoid(x)
    o_ref[...] = jnp.where(lane < heads, pre, jnp.where(lane < 2 * heads, suf, beta))


def _gates_call(p, col_block, alog_row, dtb_row, heads):
    b, n, _ = p.shape
    return pl.pallas_call(
        functools.partial(_gates_kernel, heads=heads),
        grid=(b,),
        in_specs=[
            pl.BlockSpec((None, n, LANES), lambda i: (i, 0, col_block)),
            pl.BlockSpec((1, LANES), lambda i: (0, 0)),
            pl.BlockSpec((1, LANES), lambda i: (0, 0)),
        ],
        out_specs=pl.BlockSpec((None, n, LANES), lambda i: (i, 0, 0)),
        out_shape=jax.ShapeDtypeStruct((b, n, LANES), F32),
        compiler_params=_params("arbitrary"),
        name="gdn_gates",
    )(p, alog_row, dtb_row)


def _l2norm(t):
    return t * lax.rsqrt(jnp.sum(t * t, axis=-1, keepdims=True) + EPS)


def _dot_nt(a, b):
    return lax.dot_general(a, b, (((1,), (1,)), ((), ())), preferred_element_type=F32)


def _dot_tn(a, b):
    return lax.dot_general(a, b, (((0,), (0,)), ((), ())), preferred_element_type=F32)


def _dot(a, b):
    return jnp.dot(a, b, preferred_element_type=F32)


def _tri_masks():
    c = GDN_CHUNK
    r = np.arange(c)[:, None]
    col = np.arange(2 * c)[None, :]
    j = col % c
    bwd = col >= c
    hi = np.where(bwd, j, r)
    lo = np.where(bwd, r, j)
    masks = [hi >= lo, hi > lo, r == j]
    s = 1
    while s < c:
        masks.append(((r ^ j) < 2 * s) & ((hi & s) != 0) & ((lo & s) == 0))
        s *= 2
    return np.stack(masks).astype(np.float32)


def _twice_diag(x16):
    z = jnp.zeros_like(x16)
    return jnp.concatenate([jnp.concatenate([x16, z], axis=1), jnp.concatenate([z, x16], axis=1)], axis=0)


def _gdn_kernel(qx_ref, kx_ref, vx_ref, zx_ref, qc_ref, kc_ref, vc_ref, gcol_ref, grow_ref, mask_ref, m16_ref,
                wq_ref, wk_ref, wv_ref, og_ref, y_ref,
                q_s, k_s, v_s, u_s, w_s, qd_s, kd_s, qkm_s, dl_s, o_s, st_s, pad_s,
                *, heads, hg, chunk_group, n_units):
    t_step = pl.program_id(0)
    groups = heads // hg
    hblk = jnp.minimum(t_step, n_units - 1) % groups
    slot_w = t_step % 2
    slot_s = 1 - slot_w
    n_x = qx_ref.shape[0]
    n_c = qc_ref.shape[0]
    c = GDN_CHUNK
    nc_c, nc_x = n_c // c, n_x // c
    n_chunks = nc_c + nc_x
    n_tot = n_c + n_x
    n_levels = mask_ref.shape[0] - 3
    halo = SUBLANES

    def stage1():
        pad_s[0:halo, :] = jnp.zeros((halo, LANES), F32)
        for hh in range(hg):
            lanes = slice(hh * LANES, (hh + 1) * LANES)
            for src_c, src_x, dst, w_ref, kind in ((qc_ref, qx_ref, q_s, wq_ref, "q"),
                                                   (kc_ref, kx_ref, k_s, wk_ref, "k"),
                                                   (vc_ref, vx_ref, v_s, wv_ref, "v")):
                w = w_ref[:, lanes]
                for src, base, n in ((src_c, 0, n_c), (src_x, n_c, n_x)):
                    rb = math.gcd(n, 256)

                    def copy_rows(i, carry, src=src, lanes=lanes, rb=rb):
                        r0 = pl.multiple_of(i * rb, rb)
                        pad_s[pl.ds(pl.multiple_of(halo + r0, halo), rb), :] = src[pl.ds(r0, rb), lanes].astype(F32)
                        return carry

                    def conv_rows(i, carry, base=base, w=w, kind=kind, dst=dst, hh=hh, rb=rb):
                        r0 = pl.multiple_of(i * rb, rb)
                        first = halo - (SHORT_CONV - 1) // 2 + r0
                        t = w[0:1, :] * pad_s[pl.ds(first, rb), :]
                        for tap in range(1, SHORT_CONV):
                            t = t + w[tap:tap + 1, :] * pad_s[pl.ds(first + tap, rb), :]
                        t = _silu(t)
                        if kind == "q":
                            t = _l2norm(t) * (HEAD_DIM ** -0.5)
                        elif kind == "k":
                            t = _l2norm(t)
                        dst[hh, pl.ds(pl.multiple_of(base + r0, halo), rb), :] = t
                        return carry

                    lax.fori_loop(0, n // rb, copy_rows, 0)
                    pad_s[halo + n:2 * halo + n, :] = jnp.zeros((halo, LANES), F32)
                    lax.fori_loop(0, n // rb, conv_rows, 0, unroll=min(4, n // rb))

    glane = lax.broadcasted_iota(jnp.int32, (c, LANES), 1)
    left = glane < c

    def block_diag(x):
        x16 = x.astype(BF16)
        return jnp.concatenate([x16 * m16_ref[0], x16 * m16_ref[1]], axis=0)

    def wy_group(it):
        items = []
        for hh in range(hg):
            for j in range(chunk_group):
                ch = it * chunk_group + j
                rows = pl.ds(pl.multiple_of(ch * c, c), c)
                head = hblk * hg + hh
                blk = gcol_ref[rows, :]

                def col(idx, blk=blk):
                    return jnp.broadcast_to(jnp.sum(jnp.where(glane == idx, blk, 0.0), axis=-1, keepdims=True), (c, LANES))

                k = k_s[hh, rows, :]
                k16 = k.astype(BF16)
                items.append(dict(hh=hh, ch=ch, rows=rows, k=k, k16=k16, q=q_s[hh, rows, :], v=v_s[hh, rows, :],
                                  gf=col(head), gb=col(heads + head), bf=col(2 * heads + head),
                                  bb=col(3 * heads + head), rp=grow_ref[hh, ch]))
        raws = [_dot_nt(jnp.concatenate([d["k16"], d["q"].astype(BF16)], axis=0),
                        jnp.concatenate([d["k16"], d["k16"]], axis=0)) for d in items]
        yield
        for d, raw in zip(items, raws):
            gam_col = jnp.where(left, d["gf"], d["gb"])
            beta_col = jnp.where(left, d["bf"], d["bb"])
            incl = mask_ref[0]
            diff = gam_col - d["rp"][0:1, :]
            decay = jnp.exp(jnp.where(incl > 0.0, diff, 0.0)) * incl
            a = raw[:c] * decay * beta_col * mask_ref[1]
            d["a16"] = a.astype(BF16)
            qkm_s[slot_w, d["hh"], d["ch"]] = (raw[c:] * decay).astype(BF16)
            d["t"] = mask_ref[2] - a * mask_ref[3]
        for lvl in range(1, n_levels):
            xs = [_dot(d["a16"] * m16_ref[2 + lvl], block_diag(d["t"])) for d in items]
            yield
            ys = [_dot(d["t"].astype(BF16), block_diag(x)) for d, x in zip(items, xs)]
            yield
            for d, y in zip(items, ys):
                d["t"] = d["t"] - y
        us, ws = [], []
        for d in items:
            beta_row = d["rp"][1:2, :]
            scale_w = beta_row * jnp.exp(d["rp"][0:1, :])
            us.append(_dot((d["t"] * beta_row).astype(BF16), _twice_diag(d["v"].astype(BF16))))
            ws.append(_dot((d["t"] * scale_w).astype(BF16), _twice_diag(d["k16"])))
        yield
        for d, u, w in zip(items, us, ws):
            hh, ch, rows = d["hh"], d["ch"], d["rows"]
            for dirn, gam in ((0, d["gf"]), (1, d["gb"])):
                lanes = slice(dirn * LANES, (dirn + 1) * LANES)
                g_last = gam[c - 1:c, :] if dirn == 0 else gam[0:1, :]
                u_s[slot_w, hh, dirn, rows, :] = u[:, lanes].astype(BF16)
                w_s[slot_w, hh, dirn, rows, :] = w[:, lanes].astype(BF16)
                qd_s[slot_w, hh, dirn, rows, :] = (d["q"] * jnp.exp(gam)).astype(BF16)
                kd_s[slot_w, hh, dirn, rows, :] = (d["k"] * jnp.exp(g_last - gam)).astype(BF16)
                dl_s[slot_w, hh, dirn, ch] = jnp.broadcast_to(jnp.exp(g_last), (SUBLANES, LANES))

    chains = [(hh, dirn) for hh in range(hg) for dirn in range(2)]

    def scan_step(s, states):
        ch_b = jnp.where(s < nc_c, nc_c - 1 - s, n_chunks + nc_c - 1 - s)
        chunk = [s if dirn == 0 else ch_b for hh, dirn in chains]
        rows = [pl.ds(pl.multiple_of(ch * c, c), c) for ch in chunk]
        s16 = [st.astype(BF16) for st in states]
        r1 = [_dot(jnp.concatenate([w_s[slot_s, hh, dirn, rw, :], qd_s[slot_s, hh, dirn, rw, :]], axis=0), st)
              for (hh, dirn), rw, st in zip(chains, rows, s16)]
        yield
        vn16 = [(u_s[slot_s, hh, dirn, rw, :].astype(F32) - r[:c]).astype(BF16)
                for (hh, dirn), rw, r in zip(chains, rows, r1)]
        upd = [_dot_tn(kd_s[slot_s, hh, dirn, rw, :], vn) for (hh, dirn), rw, vn in zip(chains, rows, vn16)]
        zero = jnp.zeros((c, LANES), BF16)
        intra = [_dot(qkm_s[slot_s, hh, ch], jnp.concatenate([vn, zero] if dirn == 0 else [zero, vn], axis=0))
                 for (hh, dirn), ch, vn in zip(chains, chunk, vn16)]
        yield
        for (hh, dirn), rw, r, o in zip(chains, rows, r1, intra):
            o_s[hh, rw, :] += r[c:] + o
        states[:] = [st * dl_s[slot_s, hh, dirn, ch][0:1, :] + up
                     for (hh, dirn), ch, st, up in zip(chains, chunk, states, upd)]

    def scan_group(it):
        states = [st_s[i] for i in range(len(chains))]
        for j in range(chunk_group):
            yield from scan_step(it * chunk_group + j, states)
        for i, st in enumerate(states):
            st_s[i] = st

    def interleave(*gens):
        live = [[gen, 0, n_stages] for gen, n_stages in gens]
        while live:
            live.sort(key=lambda e: e[1] / e[2])
            entry = live[0]
            try:
                next(entry[0])
                entry[1] += 1
            except StopIteration:
                live.remove(entry)

    def scan_init():
        for i in range(len(chains)):
            st_s[i] = jnp.zeros((HEAD_DIM, HEAD_DIM), F32)
        zrows = math.gcd(n_tot, 256)

        def zero_rows(i, carry):
            for hh in range(hg):
                o_s[hh, pl.ds(pl.multiple_of(i * zrows, zrows), zrows), :] = jnp.zeros((zrows, LANES), F32)
            return carry

        lax.fori_loop(0, n_tot // zrows, zero_rows, 0)

    def stage4():
        rb = math.gcd(n_x, 256)
        for hh in range(hg):
            lanes = slice(hh * LANES, (hh + 1) * LANES)

            def gate_rows(i, carry, hh=hh, lanes=lanes):
                rows = pl.ds(pl.multiple_of(i * rb, rb), rb)
                o = o_s[hh, pl.ds(pl.multiple_of(n_c + i * rb, math.gcd(n_c, rb)), rb), :]
                ms = jnp.mean(o * o, axis=-1, keepdims=True)
                y = o * lax.rsqrt(ms + EPS) * og_ref[...]
                y_ref[rows, lanes] = (y * _silu(zx_ref[rows, lanes].astype(F32))).astype(y_ref.dtype)
                return carry

            lax.fori_loop(0, n_x // rb, gate_rows, 0, unroll=min(4, n_x // rb))

    def run(do_wy, do_scan):
        if do_wy:
            stage1()
        if do_scan:
            scan_init()

        def group(it, carry):
            gens = []
            if do_wy:
                gens.append((wy_group(it), 2 * n_levels + 1))
            if do_scan:
                gens.append((scan_group(it), 2 * chunk_group + 1))
            interleave(*gens)
            return carry

        lax.fori_loop(0, n_chunks // chunk_group, group, 0)
        if do_scan:
            stage4()

    @pl.when(t_step == 0)
    def _():
        run(True, False)

    @pl.when(jnp.logical_and(t_step > 0, t_step < n_units))
    def _():
        run(True, True)

    @pl.when(t_step == n_units)
    def _():
        run(False, True)


def _gdn_call(px, pc, gcol, grow, conv_w, onorm_g, heads, blk_q, blk_k, blk_v, blk_z):
    b, n_x, _ = px.shape
    n_c = pc.shape[1]
    c = GDN_CHUNK
    n_tot = n_c + n_x
    n_chunks = n_tot // c
    hg = GDN_HEADS_PER_STEP if heads % GDN_HEADS_PER_STEP == 0 else 1
    chunk_group = max(g for g in range(1, GDN_CHUNK_GROUP + 1) if n_chunks % g == 0)
    assert all(blk % hg == 0 for blk in (blk_q, blk_k, blk_v, blk_z))
    wide = hg * LANES
    groups = heads // hg
    n_units = b * groups
    tri = _tri_masks()
    masks = jnp.asarray(tri)
    lane_half = np.arange(2 * c)[None, :] < c
    half_masks = np.stack([np.broadcast_to(lane_half, (c, 2 * c)), np.broadcast_to(~lane_half, (c, 2 * c))])
    masks16 = jnp.asarray(np.concatenate([half_masks.astype(np.float32), tri[3:]]), dtype=BF16)

    def build(t):
        u = jnp.minimum(t, n_units - 1)
        return u // groups, u % groups

    def finish(t):
        u = jnp.maximum(t - 1, 0)
        return u // groups, u % groups

    def seq_spec(n, blk0, which):
        return pl.BlockSpec((None, n, wide), lambda t: (which(t)[0], 0, blk0 // hg + which(t)[1]))

    def wspec(blk0):
        return pl.BlockSpec((SHORT_CONV, wide), lambda t: (0, blk0 // hg + build(t)[1]))

    in_specs = [
        seq_spec(n_x, blk_q, build), seq_spec(n_x, blk_k, build), seq_spec(n_x, blk_v, build),
        seq_spec(n_x, blk_z, finish),
        seq_spec(n_c, blk_q, build), seq_spec(n_c, blk_k, build), seq_spec(n_c, blk_v, build),
        pl.BlockSpec((None, n_tot, LANES), lambda t: (build(t)[0], 0, 0)),
        pl.BlockSpec((None, hg, n_chunks, SUBLANES, LANES), lambda t: (build(t)[0], build(t)[1], 0, 0, 0)),
        pl.BlockSpec(masks.shape, lambda t: (0, 0, 0)),
        pl.BlockSpec(masks16.shape, lambda t: (0, 0, 0)),
        wspec(0), wspec(heads), wspec(2 * heads),
        pl.BlockSpec((1, LANES), lambda t: (0, 0)),
    ]
    per_dir = pltpu.VMEM((2, hg, 2, n_tot, LANES), BF16)
    scratch = [pltpu.VMEM((hg, n_tot, LANES), F32)] * 3 + [
        per_dir, per_dir, per_dir, per_dir,
        pltpu.VMEM((2, hg, n_chunks, c, 2 * c), BF16),
        pltpu.VMEM((2, hg, 2, n_chunks, SUBLANES, LANES), F32),
        pltpu.VMEM((hg, n_tot, LANES), F32),
        pltpu.VMEM((2 * hg, HEAD_DIM, HEAD_DIM), F32),
        pltpu.VMEM((max(n_x, n_c) + 2 * SUBLANES, LANES), F32),
    ]
    return pl.pallas_call(
        functools.partial(_gdn_kernel, heads=heads, hg=hg, chunk_group=chunk_group, n_units=n_units),
        grid=(n_units + 1,),
        in_specs=in_specs,
        out_specs=pl.BlockSpec((None, n_x, wide), lambda t: (finish(t)[0], 0, finish(t)[1])),
        out_shape=jax.ShapeDtypeStruct((b, n_x, heads * HEAD_DIM), BF16),
        scratch_shapes=scratch,
        compiler_params=_params("arbitrary"),
        name="gdn",
    )(px, px, px, px, pc, pc, pc, gcol, grow, masks, masks16, conv_w, conv_w, conv_w, onorm_g.reshape(1, LANES))


def _pool_kernel(p_ref, w_ref, s_ref, o_ref, pad_s):
    n = p_ref.shape[0]
    gc = w_ref.shape[1]
    slabs = pad_s.shape[0]
    halo = pad_s.shape[1] - n
    half = halo // 2
    for sl in range(slabs):
        pad_s[sl, 0:half, :] = jnp.zeros((half, LANES), F32)
        pad_s[sl, half + n:halo + n, :] = jnp.zeros((half, LANES), F32)
    rb = math.gcd(n, 256)
    for gi, r in enumerate(POOL_RADII):
        cols = slice(gi * gc, (gi + 1) * gc)

        def copy_rows(i, carry):
            r0 = pl.multiple_of(i * rb, rb)
            for sl in range(slabs):
                lanes = slice(gi * gc + sl * LANES, gi * gc + (sl + 1) * LANES)
                pad_s[sl, pl.ds(pl.multiple_of(half + r0, SUBLANES), rb), :] = p_ref[pl.ds(r0, rb), lanes].astype(F32)
            return carry

        def pool_rows(i, carry):
            r0 = pl.multiple_of(i * rb, rb)
            tr = lax.broadcasted_iota(jnp.int32, (rb, 1), 0) + r0
            inv_cnt = 1.0 / (jnp.minimum(tr + r + 1, n) - jnp.maximum(tr - r, 0)).astype(F32)
            parts = []
            for sl in range(slabs):
                tot = pad_s[sl, pl.ds(half - r + r0, rb), :]
                for s in range(1 - r, r + 1):
                    tot = tot + pad_s[sl, pl.ds(half + s + r0, rb), :]
                centre = pad_s[sl, pl.ds(pl.multiple_of(half + r0, SUBLANES), rb), :]
                parts.append((tot * inv_cnt - centre).astype(BF16))
            y = jnp.dot(jnp.concatenate(parts, axis=1), w_ref[gi], preferred_element_type=F32)
            o_ref[pl.ds(r0, rb), cols] = (y * s_ref[:, cols]).astype(o_ref.dtype)
            return carry

        lax.fori_loop(0, n // rb, copy_rows, 0)
        lax.fori_loop(0, n // rb, pool_rows, 0, unroll=min(2, n // rb))


def _pool_call(px, col_block, pool_w, pool_scale):
    b, n, _ = px.shape
    g, gc, _ = pool_w.shape
    width = g * gc
    return pl.pallas_call(
        _pool_kernel,
        grid=(b,),
        in_specs=[
            pl.BlockSpec((None, n, width), lambda i: (i, 0, col_block)),
            pl.BlockSpec((g, gc, gc), lambda i: (0, 0, 0)),
            pl.BlockSpec((1, width), lambda i: (0, 0)),
        ],
        out_specs=pl.BlockSpec((None, n, width), lambda i: (i, 0, 0)),
        out_shape=jax.ShapeDtypeStruct((b, n, width), BF16),
        scratch_shapes=[pltpu.VMEM((gc // LANES, n + 2 * SUBLANES * pl.cdiv(max(POOL_RADII), SUBLANES), LANES), F32)],
        compiler_params=_params("arbitrary"),
        name="pool",
    )(px, pool_w, pool_scale.reshape(1, width))


def _sgu_kernel(u_ref, v_ref, lg_ref, lb_ref, ws_ref, bs_ref, o_ref, vn_ref):
    tr, wdt = v_ref.shape
    gw = wdt // SGU_GROUPS
    rc = 64

    def ln_body(r, carry):
        rows = pl.ds(pl.multiple_of(r * rc, rc), rc)
        v = v_ref[rows, :].astype(F32)
        mu = jnp.mean(v, axis=-1, keepdims=True)
        d = v - mu
        var = jnp.mean(d * d, axis=-1, keepdims=True)
        vn_ref[rows, :] = (d * lax.rsqrt(var + EPS) * lg_ref[...] + lb_ref[...]).astype(BF16)
        return carry

    lax.fori_loop(0, tr // rc, ln_body, 0, unroll=2)
    for ch in range(tr // SGU_CHUNK):
        rows = slice(ch * SGU_CHUNK, (ch + 1) * SGU_CHUNK)
        for g in range(SGU_GROUPS):
            cols = slice(g * gw, (g + 1) * gw)
            mixed = jnp.dot(ws_ref[g], vn_ref[rows, cols], preferred_element_type=F32) + bs_ref[:, g:g + 1]
            o_ref[rows, cols] = (u_ref[rows, cols].astype(F32) * mixed).astype(o_ref.dtype)


def _sgu_call(zz, ln_g, ln_b, ws, bs_t, tr):
    b, n, w2 = zz.shape
    wdt = w2 // 2
    return pl.pallas_call(
        _sgu_kernel,
        grid=(b, n // tr),
        in_specs=[
            pl.BlockSpec((None, tr, wdt), lambda i, r: (i, r, 0)),
            pl.BlockSpec((None, tr, wdt), lambda i, r: (i, r, 1)),
            pl.BlockSpec((1, wdt), lambda i, r: (0, 0)),
            pl.BlockSpec((1, wdt), lambda i, r: (0, 0)),
            pl.BlockSpec((SGU_GROUPS, SGU_CHUNK, SGU_CHUNK), lambda i, r: (0, 0, 0)),
            pl.BlockSpec((SGU_CHUNK, SGU_GROUPS), lambda i, r: (0, 0)),
        ],
        out_specs=pl.BlockSpec((None, tr, wdt), lambda i, r: (i, r, 0)),
        out_shape=jax.ShapeDtypeStruct((b, n, wdt), BF16),
        scratch_shapes=[pltpu.VMEM((tr, wdt), BF16)],
        compiler_params=_params("arbitrary", "arbitrary"),
        name="sgu",
    )(zz, zz, ln_g.reshape(1, wdt), ln_b.reshape(1, wdt), ws, bs_t)


def _pos_kernel(r_ref, c_ref, o_ref):
    half = r_ref.shape[-1]
    o_ref[:, :half] = jnp.broadcast_to(r_ref[...], (o_ref.shape[0], half))
    o_ref[:, half:] = c_ref[...]


def _grid_pos_embed(n, d):
    rows = n // GRID_W
    nf = d // 4
    omega = 1.0 / (10000.0 ** (jnp.arange(nf, dtype=F32) / nf))
    ar = jnp.arange(rows, dtype=F32)[:, None] * omega
    ac = jnp.arange(GRID_W, dtype=F32)[:, None] * omega
    tab_r = jnp.concatenate([jnp.sin(ar), jnp.cos(ar)], axis=-1).reshape(rows, 1, 2 * nf)
    tab_c = jnp.concatenate([jnp.sin(ac), jnp.cos(ac)], axis=-1)
    return pl.pallas_call(
        _pos_kernel,
        grid=(rows,),
        in_specs=[pl.BlockSpec((None, 1, 2 * nf), lambda i: (i, 0, 0)),
                  pl.BlockSpec((GRID_W, 2 * nf), lambda i: (0, 0))],
        out_specs=pl.BlockSpec((GRID_W, d), lambda i: (i, 0)),
        out_shape=jax.ShapeDtypeStruct((n, d), F32),
        compiler_params=_params("arbitrary"),
        name="pos_table",
    )(tab_r, tab_c)


def _pick_tile(n, candidates):
    for t in candidates:
        if n % t == 0:
            return t
    raise ValueError(f"no tile for {n}")


def _gate_rows(gcol, heads):
    b, n, _ = gcol.shape
    c = GDN_CHUNK
    g = gcol[:, :, :4 * heads].reshape(b, n // c, c, 2, 2, heads)
    g = jnp.transpose(g, (0, 5, 1, 3, 4, 2)).reshape(b, heads, n // c, 2, 2 * c)
    return jnp.pad(g, ((0, 0), (0, 0), (0, 0), (0, SUBLANES - 2), (0, 0)))


def _mlp(h, mods, layer, norm_g, w1, w2, final_g, n_seq, batch_row):
    tm = _pick_tile(n_seq, (1024, 512, 256, 128))
    hid = _nm_call(h, None, norm_g, mods, layer, 3, batch_row(tm), w1, act="relu2", out_dtype=BF16,
                   tm=tm, tn=_pick_tile(w1.shape[1], (2048, 1024, 512, 256, 128)), rows_per_seq=n_seq,
                   name=f"mlp_up{layer}")
    tall = final_g is None
    tm2 = _pick_tile(n_seq, (1024, 512, 256, 128) if tall else (512, 256, 128))
    tk = _pick_tile(w2.shape[0], (1024, 512, 256, 128) if tall else (2048, 1024, 512, 256, 128))
    return _mr_call(hid, w2, h, None, mods, layer, 5, final_g, tm=tm2, tk=tk, rows_per_seq=n_seq,
                    name=f"mlp_down{layer}")


def kernel(x, c, ctx, c_ctx, ada_w, ada_b, norm1_g, norm2_g, mlp_w1, mlp_w2, ev_w_in, ev_conv_w, ev_a_log,
           ev_dt_bias, ev_onorm_g, ev_pool_w, ev_pool_scale, ev_w_out, od_w_in, od_ln_g, od_ln_b, od_ws, od_bs,
           od_w_out, final_g):
    b, n_seq, d = x.shape
    n_ctx = ctx.shape[1]
    depth = ada_w.shape[0]
    heads = ev_a_log.shape[-1]
    qk_w = heads * HEAD_DIM
    gdn_qkv = 3 * qk_w
    gdn_in = gdn_qkv + 4 * heads
    pool_w_dim = ev_pool_scale.shape[-1]
    assert b + 1 <= MOD_ROWS and 4 * heads <= LANES and n_seq % GDN_CHUNK == 0 and n_ctx % GDN_CHUNK == 0
    assert qk_w % 1024 == 0 and pool_w_dim == qk_w

    cc = jnp.zeros((MOD_ROWS, d), F32).at[:b].set(c).at[b].set(c_ctx)
    mods = _ada_call(cc, ada_w, ada_b).reshape(depth, MOD_ROWS, N_ADA, 1, d)

    def batch_row(tm):
        return lambda i: (i * tm) // n_seq

    pos = _grid_pos_embed(n_seq, d)
    h = x.reshape(b * n_seq, d)
    last_even = 2 * ((depth - 1) // 2)
    assert last_even == 0, "context stream is only advanced through its first DeltaNet layer"

    for i in range(depth):
        j = i // 2
        fin = final_g if i == depth - 1 else None
        if i % 2 == 0:
            w_in = _cast_call(ev_w_in, j)
            ab_pad = LANES - 4 * heads
            n_cols = 5 * qk_w + LANES
            n_pad = -n_cols % 768
            w_x = jnp.concatenate([
                w_in[:, :gdn_qkv], w_in[:, gdn_in:gdn_in + 2 * qk_w], w_in[:, gdn_qkv:gdn_in],
                jnp.zeros((d, ab_pad + n_pad), BF16)], axis=1)
            w_c = jnp.concatenate([w_in[:, :gdn_in], jnp.zeros((d, ab_pad), BF16)], axis=1)
            blk = qk_w // LANES
            tm = _pick_tile(n_seq, (1024, 512, 256, 128))
            px, gate_x = _nm_call(h, pos, norm1_g[i], mods, i, 0, batch_row(tm), w_x, act="none", out_dtype=BF16,
                                  tm=tm, tn=768, rows_per_seq=n_seq, name="even_in", keep_col=5 * qk_w)
            tmc = _pick_tile(b * n_ctx, (1024, 512, 256, 128))
            pc, gate_c = _nm_call(ctx.reshape(b * n_ctx, d), None, norm1_g[i], mods, i, 0, lambda t: b, w_c,
                                  act="none", out_dtype=BF16, tm=tmc, tn=_pick_tile(w_c.shape[1], (640, 128)),
                                  rows_per_seq=b * n_ctx, name="even_in_ctx", keep_col=gdn_qkv)
            px = px.reshape(b, n_seq, -1)
            pc = pc.reshape(b, n_ctx, -1)

            lane_pad = jnp.zeros((LANES - 2 * heads,), F32)
            alog_row = jnp.concatenate([ev_a_log[j].reshape(-1), lane_pad]).reshape(1, LANES)
            dtb_row = jnp.concatenate([ev_dt_bias[j].reshape(-1), lane_pad]).reshape(1, LANES)
            gcol = jnp.concatenate([_gates_call(gate_c.reshape(b, n_ctx, LANES), 0, alog_row, dtb_row, heads),
                                    _gates_call(gate_x.reshape(b, n_seq, LANES), 0, alog_row, dtb_row, heads)],
                                   axis=1)
            y_a = _gdn_call(px, pc, gcol, _gate_rows(gcol, heads), ev_conv_w[j], ev_onorm_g[j],
                            heads, 0, blk, 2 * blk, 3 * blk)
            y_b = _pool_call(px, 4 * qk_w // pool_w_dim, ev_pool_w[j].astype(BF16), ev_pool_scale[j])
            y = jnp.concatenate([y_a, y_b], axis=-1).reshape(b * n_seq, -1)
            w_out = _cast_call(ev_w_out, j)
        else:
            tm = _pick_tile(n_seq, (1024, 512, 256, 128))
            w_od = _cast_call(od_w_in, j)
            zz = _nm_call(h, None, norm1_g[i], mods, i, 0, batch_row(tm), w_od, act="gelu", out_dtype=BF16,
                          tm=tm, tn=_pick_tile(w_od.shape[1], (2048, 1024, 512, 256, 128)), rows_per_seq=n_seq,
                          name="odd_in")
            y = _sgu_call(zz.reshape(b, n_seq, -1), od_ln_g[j], od_ln_b[j], od_ws[j].astype(BF16),
                          jnp.transpose(od_bs[j]), _pick_tile(n_seq, (512, 256, 128)))
            y = y.reshape(b * n_seq, -1)
            w_out = _cast_call(od_w_out, j)
        tm2 = _pick_tile(n_seq, (512, 256, 128))
        h = _mr_call(y, w_out, h, pos, mods, i, 2, None, tm=tm2,
                     tk=_pick_tile(y.shape[1], (2048, 1024, 512, 256, 128)), rows_per_seq=n_seq, name=f"mix_out{i}")
        pos = None
        h = _mlp(h, mods, i, norm2_g[i], _cast_call(mlp_w1, i), _cast_call(mlp_w2, i), fin, n_seq, batch_row)
    return h.reshape(b, n_seq, d)
```

```python
import functools
import math

import jax
import jax.numpy as jnp
import numpy as np
from jax import lax
from jax.experimental import pallas as pl
from jax.experimental.pallas import tpu as pltpu

F32 = jnp.float32
BF16 = jnp.bfloat16

EPS = 1e-6
GRID_W = 64
N_ADA = 6
HEAD_DIM = 128
GDN_CHUNK = 64
SHORT_CONV = 5
POOL_RADII = (1, 2, 4, 8)
SGU_GROUPS = 4
SGU_CHUNK = 128

LANES = 128
SUBLANES = 8
MOD_ROWS = 32
VMEM_LIMIT = 56 * 1024 * 1024

GDN_HEADS_PER_STEP = 2
GDN_CHUNK_GROUP = 9


def _params(*sem):
    return pltpu.CompilerParams(dimension_semantics=sem, vmem_limit_bytes=VMEM_LIMIT)


def _silu(x):
    return x * jax.nn.sigmoid(x)


def _cast_kernel(x_ref, o_ref):
    o_ref[...] = x_ref[...].astype(o_ref.dtype)


def _cast_call(w, layer):
    _, k, n = w.shape
    tr = _pick_tile(k, [t for t in (2048, 1024, 512, 256, 128, 64, 32, 16) if t * n * 4 <= 8 * 1024 * 1024])
    return pl.pallas_call(
        _cast_kernel,
        grid=(k // tr,),
        in_specs=[pl.BlockSpec((None, tr, n), lambda i: (layer, i, 0))],
        out_specs=pl.BlockSpec((tr, n), lambda i: (i, 0)),
        out_shape=jax.ShapeDtypeStruct((k, n), BF16),
        compiler_params=_params("arbitrary"),
        name="cast_weight",
    )(w)


def _ada_kernel(c_ref, w_ref, b_ref, o_ref):
    s = _silu(c_ref[...]).astype(BF16)
    o_ref[...] = jnp.dot(s, w_ref[...].astype(BF16), preferred_element_type=F32) + b_ref[...]


def _ada_call(cc, ada_w, ada_b):
    depth, d, n = ada_w.shape
    tn = 1024
    return pl.pallas_call(
        _ada_kernel,
        grid=(depth, n // tn),
        in_specs=[
            pl.BlockSpec((MOD_ROWS, d), lambda l, j: (0, 0)),
            pl.BlockSpec((None, d, tn), lambda l, j: (l, 0, j)),
            pl.BlockSpec((None, 1, tn), lambda l, j: (l, 0, j)),
        ],
        out_specs=pl.BlockSpec((None, MOD_ROWS, tn), lambda l, j: (l, 0, j)),
        out_shape=jax.ShapeDtypeStruct((depth, MOD_ROWS, n), F32),
        compiler_params=_params("arbitrary", "arbitrary"),
        name="ada",
    )(cc, ada_w, ada_b.reshape(depth, 1, n))


def _act(y, act):
    if act == "relu2":
        r = jnp.maximum(y, 0.0)
        return r * r
    if act == "gelu":
        return 0.5 * y * (1.0 + lax.erf(y * math.sqrt(0.5)))
    return y


def _nm_kernel(*refs, has_pos, act, row_chunk, keep):
    x_ref = refs[0]
    rest = list(refs[1:])
    pos_ref = rest.pop(0) if has_pos else None
    g_ref, sh_ref, sc_ref, w_ref, o_ref = rest[:5]
    keep_ref = rest[5] if keep is not None else None
    xn_ref = rest[-1]
    tm = x_ref.shape[0]

    @pl.when(pl.program_id(1) == 0)
    def _():
        gm = g_ref[...] * (1.0 + sc_ref[...])
        shift = sh_ref[...]

        def body(r, carry):
            rows = pl.ds(pl.multiple_of(r * row_chunk, row_chunk), row_chunk)
            x = x_ref[rows, :]
            if has_pos:
                x = x + pos_ref[rows, :]
            ms = jnp.mean(x * x, axis=-1, keepdims=True)
            xn_ref[rows, :] = (x * lax.rsqrt(ms + EPS) * gm + shift).astype(BF16)
            return carry

        lax.fori_loop(0, tm // row_chunk, body, 0, unroll=8)

    acc = jnp.dot(xn_ref[...], w_ref[...], preferred_element_type=F32)
    o_ref[...] = _act(acc, act).astype(o_ref.dtype)
    if keep is not None:
        keep_tile, keep_off = keep

        @pl.when(pl.program_id(1) == keep_tile)
        def _():
            keep_ref[...] = acc[:, keep_off:keep_off + LANES]


def _seq_major(i, tiles_per_seq, n_seqs):
    seq = i % n_seqs
    tile = i // n_seqs
    return seq * tiles_per_seq + tile, seq, tile


def _nm_call(x, pos, gain, mods, layer, k_shift, mod_row, w, *, act, out_dtype, tm, tn, rows_per_seq, name,
             keep_col=None):
    m, d = x.shape
    n = w.shape[1]
    has_pos = pos is not None
    tiles_per_seq = rows_per_seq // tm
    n_seqs = m // rows_per_seq

    def row_tile(i):
        return _seq_major(i, tiles_per_seq, n_seqs)[0] if has_pos else i

    in_specs = [pl.BlockSpec((tm, d), lambda i, j: (row_tile(i), 0))]
    args = [x]
    if has_pos:
        in_specs.append(pl.BlockSpec((tm, d), lambda i, j: (_seq_major(i, tiles_per_seq, n_seqs)[2], 0)))
        args.append(pos)
    in_specs += [
        pl.BlockSpec((1, d), lambda i, j: (0, 0)),
        pl.BlockSpec((None, None, None, 1, d), lambda i, j: (layer, mod_row(row_tile(i)), k_shift, 0, 0)),
        pl.BlockSpec((None, None, None, 1, d), lambda i, j: (layer, mod_row(row_tile(i)), k_shift + 1, 0, 0)),
        pl.BlockSpec((d, tn), lambda i, j: (0, j)),
    ]
    args += [gain.reshape(1, d), mods, mods, w]
    out_specs = [pl.BlockSpec((tm, tn), lambda i, j: (row_tile(i), j))]
    out_shape = [jax.ShapeDtypeStruct((m, n), out_dtype)]
    keep = None
    if keep_col is not None:
        keep = (keep_col // tn, keep_col % tn)
        assert keep[1] + LANES <= tn
        out_specs.append(pl.BlockSpec((tm, LANES), lambda i, j: (row_tile(i), 0)))
        out_shape.append(jax.ShapeDtypeStruct((m, LANES), F32))
    res = pl.pallas_call(
        functools.partial(_nm_kernel, has_pos=has_pos, act=act, row_chunk=16, keep=keep),
        grid=(m // tm, n // tn),
        in_specs=in_specs,
        out_specs=out_specs,
        out_shape=out_shape,
        scratch_shapes=[pltpu.VMEM((tm, d), BF16)],
        compiler_params=_params("arbitrary", "arbitrary"),
        name=name,
    )(*args)
    return res if keep_col is not None else res[0]


def _mr_kernel(*refs, n_a, has_pos, has_final, k_steps):
    a_refs = refs[:n_a]
    a_ref = a_refs[0]
    w_ref, h_ref, gate_ref = refs[n_a:n_a + 3]
    rest = list(refs[n_a + 3:])
    pos_ref = rest.pop(0) if has_pos else None
    fg_ref = rest.pop(0) if has_final else None
    o_ref, *acc = rest

    def finish(total):
        h = h_ref[...]
        if has_pos:
            h = h + pos_ref[...]
        y = h + gate_ref[...] * total
        if has_final:
            ms = jnp.mean(y * y, axis=-1, keepdims=True)
            y = y * lax.rsqrt(ms + EPS) * fg_ref[...]
        o_ref[...] = y

    if k_steps == 1:
        total, row0 = None, 0
        for ar in a_refs:
            part = jnp.dot(ar[...], w_ref[row0:row0 + ar.shape[1], :], preferred_element_type=F32)
            total = part if total is None else total + part
            row0 += ar.shape[1]
        finish(total)
        return
    acc_ref, = acc
    k = pl.program_id(1)

    @pl.when(k == 0)
    def _():
        acc_ref[...] = jnp.dot(a_ref[...], w_ref[...], preferred_element_type=F32)

    @pl.when(k > 0)
    def _():
        acc_ref[...] += jnp.dot(a_ref[...], w_ref[...], preferred_element_type=F32)

    @pl.when(k == k_steps - 1)
    def _():
        finish(acc_ref[...])


def _mr_call(a, w, h, pos, mods, layer, k_gate, final_g, *, tm, tk, rows_per_seq, name):
    slabs = tuple(a) if isinstance(a, (tuple, list)) else (a,)
    m = slabs[0].shape[0]
    kdim = sum(s.shape[1] for s in slabs)
    assert len(slabs) == 1 or tk == kdim
    d = w.shape[1]
    has_final = final_g is not None
    has_pos = pos is not None
    tiles_per_seq = rows_per_seq // tm
    n_seqs = m // rows_per_seq

    def row_tile(i):
        return _seq_major(i, tiles_per_seq, n_seqs)[0] if has_pos else i

    if len(slabs) == 1:
        in_specs = [pl.BlockSpec((tm, tk), lambda i, k: (row_tile(i), k))]
    else:
        in_specs = [pl.BlockSpec((tm, s.shape[1]), lambda i, k: (row_tile(i), 0)) for s in slabs]
    in_specs += [
        pl.BlockSpec((tk, d), lambda i, k: (k, 0)),
        pl.BlockSpec((tm, d), lambda i, k: (row_tile(i), 0)),
        pl.BlockSpec((None, None, None, 1, d), lambda i, k: (layer, row_tile(i) // tiles_per_seq, k_gate, 0, 0)),
    ]
    args = [*slabs, w, h, mods]
    if has_pos:
        in_specs.append(pl.BlockSpec((tm, d), lambda i, k: (_seq_major(i, tiles_per_seq, n_seqs)[2], 0)))
        args.append(pos)
    if has_final:
        in_specs.append(pl.BlockSpec((1, d), lambda i, k: (0, 0)))
        args.append(final_g.reshape(1, d))
    k_steps = kdim // tk
    return pl.pallas_call(
        functools.partial(_mr_kernel, n_a=len(slabs), has_pos=has_pos, has_final=has_final, k_steps=k_steps),
        grid=(m // tm, k_steps),
        in_specs=in_specs,
        out_specs=pl.BlockSpec((tm, d), lambda i, k: (row_tile(i), 0)),
        out_shape=jax.ShapeDtypeStruct((m, d), F32),
        scratch_shapes=[pltpu.VMEM((tm, d), F32)] if k_steps > 1 else [],
        compiler_params=_params("arbitrary", "arbitrary"),
        name=name,
    )(*args)


def _gates_kernel(ab_ref, alog_ref, dtb_ref, o_ref, *, heads):
    x = ab_ref[...]
    n = x.shape[0]
    lane = lax.broadcasted_iota(jnp.int32, x.shape, 1)
    row = lax.broadcasted_iota(jnp.int32, x.shape, 0)
    pos = row % GDN_CHUNK
    g = -jnp.exp(alog_ref[...]) * jax.nn.softplus(x + dtb_ref[...])
    pre = g
    suf = g
    s = 1
    while s < GDN_CHUNK:
        pre = pre + jnp.where(pos >= s, pltpu.roll(pre, s, 0), 0.0)
        suf = suf + jnp.where(pos < GDN_CHUNK - s, pltpu.roll(suf, n - s, 0), 0.0)
        s *= 2
    beta = jax.nn.sigmoid(x)
    o_ref[...] = jnp.where(lane < heads, pre, jnp.where(lane < 2 * heads, suf, beta))


def _gates_call(p, col_block, alog_row, dtb_row, heads):
    b, n, _ = p.shape
    return pl.pallas_call(
        functools.partial(_gates_kernel, heads=heads),
        grid=(b,),
        in_specs=[
            pl.BlockSpec((None, n, LANES), lambda i: (i, 0, col_block)),
            pl.BlockSpec((1, LANES), lambda i: (0, 0)),
            pl.BlockSpec((1, LANES), lambda i: (0, 0)),
        ],
        out_specs=pl.BlockSpec((None, n, LANES), lambda i: (i, 0, 0)),
        out_shape=jax.ShapeDtypeStruct((b, n, LANES), F32),
        compiler_params=_params("arbitrary"),
        name="gdn_gates",
    )(p, alog_row, dtb_row)


def _l2norm(t):
    return t * lax.rsqrt(jnp.sum(t * t, axis=-1, keepdims=True) + EPS)


def _dot_nt(a, b):
    return lax.dot_general(a, b, (((1,), (1,)), ((), ())), preferred_element_type=F32)


def _dot_tn(a, b):
    return lax.dot_general(a, b, (((0,), (0,)), ((), ())), preferred_element_type=F32)


def _dot(a, b):
    return jnp.dot(a, b, preferred_element_type=F32)


def _tri_masks():
    c = GDN_CHUNK
    r = np.arange(c)[:, None]
    col = np.arange(2 * c)[None, :]
    j = col % c
    bwd = col >= c
    hi = np.where(bwd, j, r)
    lo = np.where(bwd, r, j)
    masks = [hi >= lo, hi > lo, r == j]
    s = 1
    while s < c:
        masks.append(((r ^ j) < 2 * s) & ((hi & s) != 0) & ((lo & s) == 0))
        s *= 2
    return np.stack(masks).astype(np.float32)


def _twice_diag(x16):
    z = jnp.zeros_like(x16)
    return jnp.concatenate([jnp.concatenate([x16, z], axis=1), jnp.concatenate([z, x16], axis=1)], axis=0)


def _gdn_kernel(qx_ref, kx_ref, vx_ref, zx_ref, qc_ref, kc_ref, vc_ref, gcol_ref, grow_ref, mask_ref, m16_ref,
                wq_ref, wk_ref, wv_ref, og_ref, y_ref,
                q_s, k_s, v_s, u_s, w_s, qd_s, kd_s, qkm_s, dl_s, o_s, st_s, pad_s,
                *, heads, hg, chunk_group, n_units):
    t_step = pl.program_id(0)
    groups = heads // hg
    hblk = jnp.minimum(t_step, n_units - 1) % groups
    slot_w = t_step % 2
    slot_s = 1 - slot_w
    n_x = qx_ref.shape[0]
    n_c = qc_ref.shape[0]
    c = GDN_CHUNK
    nc_c, nc_x = n_c // c, n_x // c
    n_chunks = nc_c + nc_x
    n_tot = n_c + n_x
    n_levels = mask_ref.shape[0] - 3
    halo = SUBLANES

    def stage1():
        pad_s[0:halo, :] = jnp.zeros((halo, LANES), F32)
        for hh in range(hg):
            lanes = slice(hh * LANES, (hh + 1) * LANES)
            for src_c, src_x, dst, w_ref, kind in ((qc_ref, qx_ref, q_s, wq_ref, "q"),
                                                   (kc_ref, kx_ref, k_s, wk_ref, "k"),
                                                   (vc_ref, vx_ref, v_s, wv_ref, "v")):
                w = w_ref[:, lanes]
                for src, base, n in ((src_c, 0, n_c), (src_x, n_c, n_x)):
                    rb = math.gcd(n, 256)

                    def copy_rows(i, carry, src=src, lanes=lanes, rb=rb):
                        r0 = pl.multiple_of(i * rb, rb)
                        pad_s[pl.ds(pl.multiple_of(halo + r0, halo), rb), :] = src[pl.ds(r0, rb), lanes].astype(F32)
                        return carry

                    def conv_rows(i, carry, base=base, w=w, kind=kind, dst=dst, hh=hh, rb=rb):
                        r0 = pl.multiple_of(i * rb, rb)
                        first = halo - (SHORT_CONV - 1) // 2 + r0
                        t = w[0:1, :] * pad_s[pl.ds(first, rb), :]
                        for tap in range(1, SHORT_CONV):
                            t = t + w[tap:tap + 1, :] * pad_s[pl.ds(first + tap, rb), :]
                        t = _silu(t)
                        if kind == "q":
                            t = _l2norm(t) * (HEAD_DIM ** -0.5)
                        elif kind == "k":
                            t = _l2norm(t)
                        dst[hh, pl.ds(pl.multiple_of(base + r0, halo), rb), :] = t
                        return carry

                    lax.fori_loop(0, n // rb, copy_rows, 0)
                    pad_s[halo + n:2 * halo + n, :] = jnp.zeros((halo, LANES), F32)
                    lax.fori_loop(0, n // rb, conv_rows, 0, unroll=min(4, n // rb))

    glane = lax.broadcasted_iota(jnp.int32, (c, LANES), 1)
    left = glane < c

    def block_diag(x):
        x16 = x.astype(BF16)
        return jnp.concatenate([x16 * m16_ref[0], x16 * m16_ref[1]], axis=0)

    def wy_group(it):
        items = []
        for hh in range(hg):
            for j in range(chunk_group):
                ch = it * chunk_group + j
                rows = pl.ds(pl.multiple_of(ch * c, c), c)
                head = hblk * hg + hh
                blk = gcol_ref[rows, :]

                def col(idx, blk=blk):
                    return jnp.broadcast_to(jnp.sum(jnp.where(glane == idx, blk, 0.0), axis=-1, keepdims=True), (c, LANES))

                k = k_s[hh, rows, :]
                k16 = k.astype(BF16)
                items.append(dict(hh=hh, ch=ch, rows=rows, k=k, k16=k16, q=q_s[hh, rows, :], v=v_s[hh, rows, :],
                                  gf=col(head), gb=col(heads + head), bf=col(2 * heads + head),
                                  bb=col(3 * heads + head), rp=grow_ref[hh, ch]))
        raws = [_dot_nt(jnp.concatenate([d["k16"], d["q"].astype(BF16)], axis=0),
                        jnp.concatenate([d["k16"], d["k16"]], axis=0)) for d in items]
        yield
        for d, raw in zip(items, raws):
            gam_col = jnp.where(left, d["gf"], d["gb"])
            beta_col = jnp.where(left, d["bf"], d["bb"])
            incl = mask_ref[0]
            diff = gam_col - d["rp"][0:1, :]
            decay = jnp.exp(jnp.where(incl > 0.0, diff, 0.0)) * incl
            a = raw[:c] * decay * beta_col * mask_ref[1]
            d["a16"] = a.astype(BF16)
            qkm_s[slot_w, d["hh"], d["ch"]] = (raw[c:] * decay).astype(BF16)
            d["t"] = mask_ref[2] - a * mask_ref[3]
        for lvl in range(1, n_levels):
            xs = [_dot(d["a16"] * m16_ref[2 + lvl], block_diag(d["t"])) for d in items]
            yield
            ys = [_dot(d["t"].astype(BF16), block_diag(x)) for d, x in zip(items, xs)]
            yield
            for d, y in zip(items, ys):
                d["t"] = d["t"] - y
        us, ws = [], []
        for d in items:
            beta_row = d["rp"][1:2, :]
            scale_w = beta_row * jnp.exp(d["rp"][0:1, :])
            us.append(_dot((d["t"] * beta_row).astype(BF16), _twice_diag(d["v"].astype(BF16))))
            ws.append(_dot((d["t"] * scale_w).astype(BF16), _twice_diag(d["k16"])))
        yield
        for d, u, w in zip(items, us, ws):
            hh, ch, rows = d["hh"], d["ch"], d["rows"]
            for dirn, gam in ((0, d["gf"]), (1, d["gb"])):
                lanes = slice(dirn * LANES, (dirn + 1) * LANES)
                g_last = gam[c - 1:c, :] if dirn == 0 else gam[0:1, :]
                u_s[slot_w, hh, dirn, rows, :] = u[:, lanes].astype(BF16)
                w_s[slot_w, hh, dirn, rows, :] = w[:, lanes].astype(BF16)
                qd_s[slot_w, hh, dirn, rows, :] = (d["q"] * jnp.exp(gam)).astype(BF16)
                kd_s[slot_w, hh, dirn, rows, :] = (d["k"] * jnp.exp(g_last - gam)).astype(BF16)
                dl_s[slot_w, hh, dirn, ch] = jnp.broadcast_to(jnp.exp(g_last), (SUBLANES, LANES))

    chains = [(hh, dirn) for hh in range(hg) for dirn in range(2)]

    def scan_step(s, states):
        ch_b = jnp.where(s < nc_c, nc_c - 1 - s, n_chunks + nc_c - 1 - s)
        chunk = [s if dirn == 0 else ch_b for hh, dirn in chains]
        rows = [pl.ds(pl.multiple_of(ch * c, c), c) for ch in chunk]
        s16 = [st.astype(BF16) for st in states]
        r1 = [_dot(jnp.concatenate([w_s[slot_s, hh, dirn, rw, :], qd_s[slot_s, hh, dirn, rw, :]], axis=0), st)
              for (hh, dirn), rw, st in zip(chains, rows, s16)]
        yield
        vn16 = [(u_s[slot_s, hh, dirn, rw, :].astype(F32) - r[:c]).astype(BF16)
                for (hh, dirn), rw, r in zip(chains, rows, r1)]
        upd = [_dot_tn(kd_s[slot_s, hh, dirn, rw, :], vn) for (hh, dirn), rw, vn in zip(chains, rows, vn16)]
        zero = jnp.zeros((c, LANES), BF16)
        intra = [_dot(qkm_s[slot_s, hh, ch], jnp.concatenate([vn, zero] if dirn == 0 else [zero, vn], axis=0))
                 for (hh, dirn), ch, vn in zip(chains, chunk, vn16)]
        yield
        for (hh, dirn), rw, r, o in zip(chains, rows, r1, intra):
            o_s[hh, rw, :] += r[c:] + o
        states[:] = [st * dl_s[slot_s, hh, dirn, ch][0:1, :] + up
                     for (hh, dirn), ch, st, up in zip(chains, chunk, states, upd)]

    def scan_group(it):
        states = [st_s[i] for i in range(len(chains))]
        for j in range(chunk_group):
            yield from scan_step(it * chunk_group + j, states)
        for i, st in enumerate(states):
            st_s[i] = st

    def interleave(*gens):
        live = [[gen, 0, n_stages] for gen, n_stages in gens]
        while live:
            live.sort(key=lambda e: e[1] / e[2])
            entry = live[0]
            try:
                next(entry[0])
                entry[1] += 1
            except StopIteration:
                live.remove(entry)

    def scan_init():
        for i in range(len(chains)):
            st_s[i] = jnp.zeros((HEAD_DIM, HEAD_DIM), F32)
        zrows = math.gcd(n_tot, 256)

        def zero_rows(i, carry):
            for hh in range(hg):
                o_s[hh, pl.ds(pl.multiple_of(i * zrows, zrows), zrows), :] = jnp.zeros((zrows, LANES), F32)
            return carry

        lax.fori_loop(0, n_tot // zrows, zero_rows, 0)

    def stage4():
        rb = math.gcd(n_x, 256)
        for hh in range(hg):
            lanes = slice(hh * LANES, (hh + 1) * LANES)

            def gate_rows(i, carry, hh=hh, lanes=lanes):
                rows = pl.ds(pl.multiple_of(i * rb, rb), rb)
                o = o_s[hh, pl.ds(pl.multiple_of(n_c + i * rb, math.gcd(n_c, rb)), rb), :]
                ms = jnp.mean(o * o, axis=-1, keepdims=True)
                y = o * lax.rsqrt(ms + EPS) * og_ref[...]
                y_ref[rows, lanes] = (y * _silu(zx_ref[rows, lanes].astype(F32))).astype(y_ref.dtype)
                return carry

            lax.fori_loop(0, n_x // rb, gate_rows, 0, unroll=min(4, n_x // rb))

    def run(do_wy, do_scan):
        if do_wy:
            stage1()
        if do_scan:
            scan_init()

        def group(it, carry):
            gens = []
            if do_wy:
                gens.append((wy_group(it), 2 * n_levels + 1))
            if do_scan:
                gens.append((scan_group(it), 2 * chunk_group + 1))
            interleave(*gens)
            return carry

        lax.fori_loop(0, n_chunks // chunk_group, group, 0)
        if do_scan:
            stage4()

    @pl.when(t_step == 0)
    def _():
        run(True, False)

    @pl.when(jnp.logical_and(t_step > 0, t_step < n_units))
    def _():
        run(True, True)

    @pl.when(t_step == n_units)
    def _():
        run(False, True)


def _gdn_call(px, pc, gcol, grow, conv_w, onorm_g, heads, blk_q, blk_k, blk_v, blk_z):
    b, n_x, _ = px.shape
    n_c = pc.shape[1]
    c = GDN_CHUNK
    n_tot = n_c + n_x
    n_chunks = n_tot // c
    hg = GDN_HEADS_PER_STEP if heads % GDN_HEADS_PER_STEP == 0 else 1
    chunk_group = max(g for g in range(1, GDN_CHUNK_GROUP + 1) if n_chunks % g == 0)
    assert all(blk % hg == 0 for blk in (blk_q, blk_k, blk_v, blk_z))
    wide = hg * LANES
    groups = heads // hg
    n_units = b * groups
    tri = _tri_masks()
    masks = jnp.asarray(tri)
    lane_half = np.arange(2 * c)[None, :] < c
    half_masks = np.stack([np.broadcast_to(lane_half, (c, 2 * c)), np.broadcast_to(~lane_half, (c, 2 * c))])
    masks16 = jnp.asarray(np.concatenate([half_masks.astype(np.float32), tri[3:]]), dtype=BF16)

    def build(t):
        u = jnp.minimum(t, n_units - 1)
        return u // groups, u % groups

    def finish(t):
        u = jnp.maximum(t - 1, 0)
        return u // groups, u % groups

    def seq_spec(n, blk0, which):
        return pl.BlockSpec((None, n, wide), lambda t: (which(t)[0], 0, blk0 // hg + which(t)[1]))

    def wspec(blk0):
        return pl.BlockSpec((SHORT_CONV, wide), lambda t: (0, blk0 // hg + build(t)[1]))

    in_specs = [
        seq_spec(n_x, blk_q, build), seq_spec(n_x, blk_k, build), seq_spec(n_x, blk_v, build),
        seq_spec(n_x, blk_z, finish),
        seq_spec(n_c, blk_q, build), seq_spec(n_c, blk_k, build), seq_spec(n_c, blk_v, build),
        pl.BlockSpec((None, n_tot, LANES), lambda t: (build(t)[0], 0, 0)),
        pl.BlockSpec((None, hg, n_chunks, SUBLANES, LANES), lambda t: (build(t)[0], build(t)[1], 0, 0, 0)),
        pl.BlockSpec(masks.shape, lambda t: (0, 0, 0)),
        pl.BlockSpec(masks16.shape, lambda t: (0, 0, 0)),
        wspec(0), wspec(heads), wspec(2 * heads),
        pl.BlockSpec((1, LANES), lambda t: (0, 0)),
    ]
    per_dir = pltpu.VMEM((2, hg, 2, n_tot, LANES), BF16)
    scratch = [pltpu.VMEM((hg, n_tot, LANES), F32)] * 3 + [
        per_dir, per_dir, per_dir, per_dir,
        pltpu.VMEM((2, hg, n_chunks, c, 2 * c), BF16),
        pltpu.VMEM((2, hg, 2, n_chunks, SUBLANES, LANES), F32),
        pltpu.VMEM((hg, n_tot, LANES), F32),
        pltpu.VMEM((2 * hg, HEAD_DIM, HEAD_DIM), F32),
        pltpu.VMEM((max(n_x, n_c) + 2 * SUBLANES, LANES), F32),
    ]
    return pl.pallas_call(
        functools.partial(_gdn_kernel, heads=heads, hg=hg, chunk_group=chunk_group, n_units=n_units),
        grid=(n_units + 1,),
        in_specs=in_specs,
        out_specs=pl.BlockSpec((None, n_x, wide), lambda t: (finish(t)[0], 0, finish(t)[1])),
        out_shape=jax.ShapeDtypeStruct((b, n_x, heads * HEAD_DIM), BF16),
        scratch_shapes=scratch,
        compiler_params=_params("arbitrary"),
        name="gdn",
    )(px, px, px, px, pc, pc, pc, gcol, grow, masks, masks16, conv_w, conv_w, conv_w, onorm_g.reshape(1, LANES))


def _pool_kernel(p_ref, w_ref, s_ref, o_ref, pad_s):
    n = p_ref.shape[0]
    gc = w_ref.shape[1]
    slabs = pad_s.shape[0]
    halo = pad_s.shape[1] - n
    half = halo // 2
    for sl in range(slabs):
        pad_s[sl, 0:half, :] = jnp.zeros((half, LANES), F32)
        pad_s[sl, half + n:halo + n, :] = jnp.zeros((half, LANES), F32)
    rb = math.gcd(n, 256)
    for gi, r in enumerate(POOL_RADII):
        cols = slice(gi * gc, (gi + 1) * gc)

        def copy_rows(i, carry):
            r0 = pl.multiple_of(i * rb, rb)
            for sl in range(slabs):
                lanes = slice(gi * gc + sl * LANES, gi * gc + (sl + 1) * LANES)
                pad_s[sl, pl.ds(pl.multiple_of(half + r0, SUBLANES), rb), :] = p_ref[pl.ds(r0, rb), lanes].astype(F32)
            return carry

        def pool_rows(i, carry):
            r0 = pl.multiple_of(i * rb, rb)
            tr = lax.broadcasted_iota(jnp.int32, (rb, 1), 0) + r0
            inv_cnt = 1.0 / (jnp.minimum(tr + r + 1, n) - jnp.maximum(tr - r, 0)).astype(F32)
            parts = []
            for sl in range(slabs):
                tot = pad_s[sl, pl.ds(half - r + r0, rb), :]
                for s in range(1 - r, r + 1):
                    tot = tot + pad_s[sl, pl.ds(half + s + r0, rb), :]
                centre = pad_s[sl, pl.ds(pl.multiple_of(half + r0, SUBLANES), rb), :]
                parts.append((tot * inv_cnt - centre).astype(BF16))
            y = jnp.dot(jnp.concatenate(parts, axis=1), w_ref[gi], preferred_element_type=F32)
            o_ref[pl.ds(r0, rb), cols] = (y * s_ref[:, cols]).astype(o_ref.dtype)
            return carry

        lax.fori_loop(0, n // rb, copy_rows, 0)
        lax.fori_loop(0, n // rb, pool_rows, 0, unroll=min(2, n // rb))


def _pool_call(px, col_block, pool_w, pool_scale):
    b, n, _ = px.shape
    g, gc, _ = pool_w.shape
    width = g * gc
    return pl.pallas_call(
        _pool_kernel,
        grid=(b,),
        in_specs=[
            pl.BlockSpec((None, n, width), lambda i: (i, 0, col_block)),
            pl.BlockSpec((g, gc, gc), lambda i: (0, 0, 0)),
            pl.BlockSpec((1, width), lambda i: (0, 0)),
        ],
        out_specs=pl.BlockSpec((None, n, width), lambda i: (i, 0, 0)),
        out_shape=jax.ShapeDtypeStruct((b, n, width), BF16),
        scratch_shapes=[pltpu.VMEM((gc // LANES, n + 2 * SUBLANES * pl.cdiv(max(POOL_RADII), SUBLANES), LANES), F32)],
        compiler_params=_params("arbitrary"),
        name="pool",
    )(px, pool_w, pool_scale.reshape(1, width))


def _sgu_kernel(u_ref, v_ref, lg_ref, lb_ref, ws_ref, bs_ref, o_ref, vn_ref):
    tr, wdt = v_ref.shape
    gw = wdt // SGU_GROUPS
    rc = 64

    def ln_body(r, carry):
        rows = pl.ds(pl.multiple_of(r * rc, rc), rc)
        v = v_ref[rows, :].astype(F32)
        mu = jnp.mean(v, axis=-1, keepdims=True)
        d = v - mu
        var = jnp.mean(d * d, axis=-1, keepdims=True)
        vn_ref[rows, :] = (d * lax.rsqrt(var + EPS) * lg_ref[...] + lb_ref[...]).astype(BF16)
        return carry

    lax.fori_loop(0, tr // rc, ln_body, 0, unroll=2)
    for ch in range(tr // SGU_CHUNK):
        rows = slice(ch * SGU_CHUNK, (ch + 1) * SGU_CHUNK)
        for g in range(SGU_GROUPS):
            cols = slice(g * gw, (g + 1) * gw)
            mixed = jnp.dot(ws_ref[g], vn_ref[rows, cols], preferred_element_type=F32) + bs_ref[:, g:g + 1]
            o_ref[rows, cols] = (u_ref[rows, cols].astype(F32) * mixed).astype(o_ref.dtype)


def _sgu_call(zz, ln_g, ln_b, ws, bs_t, tr):
    b, n, w2 = zz.shape
    wdt = w2 // 2
    return pl.pallas_call(
        _sgu_kernel,
        grid=(b, n // tr),
        in_specs=[
            pl.BlockSpec((None, tr, wdt), lambda i, r: (i, r, 0)),
            pl.BlockSpec((None, tr, wdt), lambda i, r: (i, r, 1)),
            pl.BlockSpec((1, wdt), lambda i, r: (0, 0)),
            pl.BlockSpec((1, wdt), lambda i, r: (0, 0)),
            pl.BlockSpec((SGU_GROUPS, SGU_CHUNK, SGU_CHUNK), lambda i, r: (0, 0, 0)),
            pl.BlockSpec((SGU_CHUNK, SGU_GROUPS), lambda i, r: (0, 0)),
        ],
        out_specs=pl.BlockSpec((None, tr, wdt), lambda i, r: (i, r, 0)),
        out_shape=jax.ShapeDtypeStruct((b, n, wdt), BF16),
        scratch_shapes=[pltpu.VMEM((tr, wdt), BF16)],
        compiler_params=_params("arbitrary", "arbitrary"),
        name="sgu",
    )(zz, zz, ln_g.reshape(1, wdt), ln_b.reshape(1, wdt), ws, bs_t)


def _pos_kernel(r_ref, c_ref, o_ref):
    half = r_ref.shape[-1]
    o_ref[:, :half] = jnp.broadcast_to(r_ref[...], (o_ref.shape[0], half))
    o_ref[:, half:] = c_ref[...]


def _grid_pos_embed(n, d):
    rows = n // GRID_W
    nf = d // 4
    omega = 1.0 / (10000.0 ** (jnp.arange(nf, dtype=F32) / nf))
    ar = jnp.arange(rows, dtype=F32)[:, None] * omega
    ac = jnp.arange(GRID_W, dtype=F32)[:, None] * omega
    tab_r = jnp.concatenate([jnp.sin(ar), jnp.cos(ar)], axis=-1).reshape(rows, 1, 2 * nf)
    tab_c = jnp.concatenate([jnp.sin(ac), jnp.cos(ac)], axis=-1)
    return pl.pallas_call(
        _pos_kernel,
        grid=(rows,),
        in_specs=[pl.BlockSpec((None, 1, 2 * nf), lambda i: (i, 0, 0)),
                  pl.BlockSpec((GRID_W, 2 * nf), lambda i: (0, 0))],
        out_specs=pl.BlockSpec((GRID_W, d), lambda i: (i, 0)),
        out_shape=jax.ShapeDtypeStruct((n, d), F32),
        compiler_params=_params("arbitrary"),
        name="pos_table",
    )(tab_r, tab_c)


def _pick_tile(n, candidates):
    for t in candidates:
        if n % t == 0:
            return t
    raise ValueError(f"no tile for {n}")


def _gate_rows(gcol, heads):
    b, n, _ = gcol.shape
    c = GDN_CHUNK
    g = gcol[:, :, :4 * heads].reshape(b, n // c, c, 2, 2, heads)
    g = jnp.transpose(g, (0, 5, 1, 3, 4, 2)).reshape(b, heads, n // c, 2, 2 * c)
    return jnp.pad(g, ((0, 0), (0, 0), (0, 0), (0, SUBLANES - 2), (0, 0)))


def _mlp(h, mods, layer, norm_g, w1, w2, final_g, n_seq, batch_row):
    tm = _pick_tile(n_seq, (1024, 512, 256, 128))
    hid = _nm_call(h, None, norm_g, mods, layer, 3, batch_row(tm), w1, act="relu2", out_dtype=BF16,
                   tm=tm, tn=_pick_tile(w1.shape[1], (2048, 1024, 512, 256, 128)), rows_per_seq=n_seq,
                   name=f"mlp_up{layer}")
    tall = final_g is None
    tm2 = _pick_tile(n_seq, (1024, 512, 256, 128) if tall else (512, 256, 128))
    tk = _pick_tile(w2.shape[0], (1024, 512, 256, 128) if tall else (2048, 1024, 512, 256, 128))
    return _mr_call(hid, w2, h, None, mods, layer, 5, final_g, tm=tm2, tk=tk, rows_per_seq=n_seq,
                    name=f"mlp_down{layer}")


def kernel(x, c, ctx, c_ctx, ada_w, ada_b, norm1_g, norm2_g, mlp_w1, mlp_w2, ev_w_in, ev_conv_w, ev_a_log,
           ev_dt_bias, ev_onorm_g, ev_pool_w, ev_pool_scale, ev_w_out, od_w_in, od_ln_g, od_ln_b, od_ws, od_bs,
           od_w_out, final_g):
    b, n_seq, d = x.shape
    n_ctx = ctx.shape[1]
    depth = ada_w.shape[0]
    heads = ev_a_log.shape[-1]
    qk_w = heads * HEAD_DIM
    gdn_qkv = 3 * qk_w
    gdn_in = gdn_qkv + 4 * heads
    pool_w_dim = ev_pool_scale.shape[-1]
    assert b + 1 <= MOD_ROWS and 4 * heads <= LANES and n_seq % GDN_CHUNK == 0 and n_ctx % GDN_CHUNK == 0
    assert qk_w % 1024 == 0 and pool_w_dim == qk_w

    cc = jnp.zeros((MOD_ROWS, d), F32).at[:b].set(c).at[b].set(c_ctx)
    mods = _ada_call(cc, ada_w, ada_b).reshape(depth, MOD_ROWS, N_ADA, 1, d)

    def batch_row(tm):
        return lambda i: (i * tm) // n_seq

    pos = _grid_pos_embed(n_seq, d)
    h = x.reshape(b * n_seq, d)
    last_even = 2 * ((depth - 1) // 2)
    assert last_even == 0, "context stream is only advanced through its first DeltaNet layer"

    for i in range(depth):
        j = i // 2
        fin = final_g if i == depth - 1 else None
        if i % 2 == 0:
            w_in = _cast_call(ev_w_in, j)
            ab_pad = LANES - 4 * heads
            n_cols = 5 * qk_w + LANES
            n_pad = -n_cols % 768
            w_x = jnp.concatenate([
                w_in[:, :gdn_qkv], w_in[:, gdn_in:gdn_in + 2 * qk_w], w_in[:, gdn_qkv:gdn_in],
                jnp.zeros((d, ab_pad + n_pad), BF16)], axis=1)
            w_c = jnp.concatenate([w_in[:, :gdn_in], jnp.zeros((d, ab_pad), BF16)], axis=1)
            blk = qk_w // LANES
            tm = _pick_tile(n_seq, (1024, 512, 256, 128))
            px, gate_x = _nm_call(h, pos, norm1_g[i], mods, i, 0, batch_row(tm), w_x, act="none", out_dtype=BF16,
                                  tm=tm, tn=768, rows_per_seq=n_seq, name="even_in", keep_col=5 * qk_w)
            tmc = _pick_tile(b * n_ctx, (1024, 512, 256, 128))
            pc, gate_c = _nm_call(ctx.reshape(b * n_ctx, d), None, norm1_g[i], mods, i, 0, lambda t: b, w_c,
                                  act="none", out_dtype=BF16, tm=tmc, tn=_pick_tile(w_c.shape[1], (640, 128)),
                                  rows_per_seq=b * n_ctx, name="even_in_ctx", keep_col=gdn_qkv)
            px = px.reshape(b, n_seq, -1)
            pc = pc.reshape(b, n_ctx, -1)

            lane_pad = jnp.zeros((LANES - 2 * heads,), F32)
            alog_row = jnp.concatenate([ev_a_log[j].reshape(-1), lane_pad]).reshape(1, LANES)
            dtb_row = jnp.concatenate([ev_dt_bias[j].reshape(-1), lane_pad]).reshape(1, LANES)
            gcol = jnp.concatenate([_gates_call(gate_c.reshape(b, n_ctx, LANES), 0, alog_row, dtb_row, heads),
                                    _gates_call(gate_x.reshape(b, n_seq, LANES), 0, alog_row, dtb_row, heads)],
                                   axis=1)
            y_a = _gdn_call(px, pc, gcol, _gate_rows(gcol, heads), ev_conv_w[j], ev_onorm_g[j],
                            heads, 0, blk, 2 * blk, 3 * blk)
            y_b = _pool_call(px, 4 * qk_w // pool_w_dim, ev_pool_w[j].astype(BF16), ev_pool_scale[j])
            y = (y_a.reshape(b * n_seq, -1), y_b.reshape(b * n_seq, -1))
            w_out = _cast_call(ev_w_out, j)
        else:
            tm = _pick_tile(n_seq, (1024, 512, 256, 128))
            w_od = _cast_call(od_w_in, j)
            zz = _nm_call(h, None, norm1_g[i], mods, i, 0, batch_row(tm), w_od, act="gelu", out_dtype=BF16,
                          tm=tm, tn=_pick_tile(w_od.shape[1], (2048, 1024, 512, 256, 128)), rows_per_seq=n_seq,
                          name="odd_in")
            y = _sgu_call(zz.reshape(b, n_seq, -1), od_ln_g[j], od_ln_b[j], od_ws[j].astype(BF16),
                          jnp.transpose(od_bs[j]), _pick_tile(n_seq, (512, 256, 128)))
            y = (y.reshape(b * n_seq, -1),)
            w_out = _cast_call(od_w_out, j)
        tm2 = _pick_tile(n_seq, (512, 256, 128))
        h = _mr_call(y, w_out, h, pos, mods, i, 2, None, tm=tm2,
                     tk=sum(s.shape[1] for s in y), rows_per_seq=n_seq, name=f"mix_out{i}")
        pos = None
        h = _mlp(h, mods, i, norm2_g[i], _cast_call(mlp_w1, i), _cast_call(mlp_w2, i), fin, n_seq, batch_row)
    return h.reshape(b, n_seq, d)
```
